```python
import math
import jax, jax.numpy as jnp
from jax import lax
import numpy as np

D_MODEL = 2048
BATCH = 4
SEQ = 2048
DEPTH = 4
DEC_BATCH = 8
DEC_SEQ = 1
PAST_LEN = 16384
PAGE_SIZE = 128

HEAD_DIM = 128
N_A = DEPTH // 2
N_B = DEPTH - N_A
D_A = 3 * D_MODEL // 4
N_HEADS = D_A // HEAD_DIM
N_KV = 4
GROUP = N_HEADS // N_KV
MEM_HEADS = 4
MEM_W = MEM_HEADS * HEAD_DIM
D_MIX = D_A + MEM_W
N_MEM = 256
D_FF = 128 * ((8 * D_MODEL // 3 + 127) // 128)
CONV_W = 3
CMP_STRIDE = 16
CMP_LEN = 2 * CMP_STRIDE
SEL_BLOCK = 64
TOPK = 16
WINDOW = 512
WIN_QBLOCK = 128
SEL_Q_CHUNK = 32
N_BRANCH = 3
N_KV_KINDS = 4
ALPHA = (2 * DEPTH) ** 0.25
BETA = (8 * DEPTH) ** -0.25
LN_EPS = 1e-5
NEG_INF = -1e30
FORCE_SCORE = 1e9

kernel_name = 'yoco_shortconv_nsa_decoder_step'


def layer_norm(x, g, b):
    xf = x.astype(jnp.float32)
    mu = xf.mean(-1, keepdims=True)
    var = jnp.square(xf - mu).mean(-1, keepdims=True)
    return ((xf - mu) * lax.rsqrt(var + LN_EPS)).astype(x.dtype) * g + b


def masked_softmax(s, mask, axis=-1):
    s = jnp.where(mask, s.astype(jnp.float32), NEG_INF)
    return jnp.where(mask, jax.nn.softmax(s, axis=axis), 0.0)


def causal_conv(v, w, prev):
    T = v.shape[1]
    full = jnp.concatenate([prev, v], axis=1)
    out = w[0] * full[:, 0:T]
    for i in range(1, CONV_W):
        out = out + w[i] * full[:, i:i + T]
    return out, full[:, T:]


def mem_attention(qm, mem_k, mem_v):
    B, T = qm.shape[:2]
    q = qm.reshape(B, T, MEM_HEADS, HEAD_DIM) * HEAD_DIM ** -0.5
    s = jnp.einsum('bthd,bmhd->bthm', q, mem_k)
    p = jax.nn.softmax(s.astype(jnp.float32), axis=-1).astype(mem_v.dtype)
    return jnp.einsum('bthm,bmhd->bthd', p, mem_v).reshape(B, T, MEM_W)


def compress_blocks(rows, pos_emb, w):
    B, T = rows.shape[:2]
    nh = T // CMP_STRIDE
    halves = rows[:, :nh * CMP_STRIDE].reshape(B, nh, CMP_STRIDE, N_KV, HEAD_DIM)
    first = jnp.einsum('bnsgd,sde->bnge', halves[:, :-1], w[:CMP_STRIDE])
    second = jnp.einsum('bnsgd,sde->bnge', halves[:, 1:], w[CMP_STRIDE:])
    pos_term = jnp.einsum('ld,lde->e', pos_emb, w)
    return first + second + pos_term


def to_sel_blocks(rows):
    B, T = rows.shape[:2]
    n = -(-T // SEL_BLOCK)
    rows = jnp.pad(rows, ((0, 0), (0, n * SEL_BLOCK - T), (0, 0), (0, 0)))
    return rows.reshape(B, n, SEL_BLOCK, N_KV, HEAD_DIM).transpose(0, 3, 1, 2, 4)


def key_side(k_cmp_rows, v_cmp_rows, k_sel_rows, v_sel_rows, cmp_pos, w_cmp):
    return (compress_blocks(k_cmp_rows, cmp_pos[0], w_cmp[0]),
            compress_blocks(v_cmp_rows, cmp_pos[1], w_cmp[1]),
            to_sel_blocks(k_sel_rows), to_sel_blocks(v_sel_rows))


def prompt_window_bands(k_rows, v_rows):
    T = k_rows.shape[1]
    nb = T // WIN_QBLOCK
    band = jnp.arange(nb)[:, None] * WIN_QBLOCK + jnp.arange(WINDOW + WIN_QBLOCK)[None, :]
    pad = ((0, 0), (WINDOW, 0), (0, 0), (0, 0))
    wk = jnp.pad(k_rows, pad)[:, band]
    wv = jnp.pad(v_rows, pad)[:, band]
    return (wk, wv, band - WINDOW)


def selected_attention(qg, idx, valid, qpos, sel_kb, sel_vb):
    B, T = qg.shape[:2]
    qc = math.gcd(T, SEL_Q_CHUNK)
    nc = T // qc

    def chunks(a):
        return jnp.moveaxis(a.reshape((B, nc, qc) + a.shape[2:]), 1, 0)

    bi = jnp.arange(B)[:, None, None, None]
    gi = jnp.arange(N_KV)[None, None, :, None]
    offs = jnp.arange(SEL_BLOCK)

    def one(args):
        q_c, idx_c, val_c, pos_c = args
        kg = sel_kb[bi, gi, idx_c]
        vg = sel_vb[bi, gi, idx_c]
        s = jnp.einsum('bqgrd,bqgkld->bqgrkl', q_c, kg)
        kpos = idx_c[..., None] * SEL_BLOCK + offs
        mask = val_c[..., None] & (kpos <= pos_c[None, :, None, None, None])
        p = masked_softmax(s, mask[:, :, :, None], axis=(-2, -1)).astype(vg.dtype)
        return jnp.einsum('bqgrkl,bqgkld->bqgrd', p, vg)

    out = lax.map(one, (chunks(qg), chunks(idx), chunks(valid), qpos.reshape(nc, qc)))
    return jnp.moveaxis(out, 0, 1).reshape(qg.shape)


def window_attention(qg, win_k, win_v, win_kpos, qpos):
    B, T = qg.shape[:2]
    nb = win_k.shape[1]
    qb = qg.reshape(B, nb, T // nb, N_KV, GROUP, HEAD_DIM)
    qp = qpos.reshape(nb, T // nb)[:, :, None]
    kp = win_kpos[:, None, :]
    mask = (kp <= qp) & (kp > qp - WINDOW) & (kp >= 0)
    s = jnp.einsum('bnqgrd,bnkgd->bnqgrk', qb, win_k)
    p = masked_softmax(s, mask[None, :, :, None, None, :]).astype(win_v.dtype)
    return jnp.einsum('bnqgrk,bnkgd->bnqgrd', p, win_v).reshape(qg.shape)


def nsa_mixer(q, gate, cmp_k, cmp_v, sel_kb, sel_vb, win_k, win_v, win_kpos, qpos):
    B, T = q.shape[:2]
    qg = q.reshape(B, T, N_KV, GROUP, HEAD_DIM) * HEAD_DIM ** -0.5
    n_cmp = cmp_k.shape[1]
    cmp_start = jnp.arange(n_cmp) * CMP_STRIDE
    cmask = (cmp_start + CMP_LEN - 1)[None, :] <= qpos[:, None]
    s = jnp.einsum('btgrd,bngd->btgrn', qg, cmp_k)
    p_cmp = masked_softmax(s, cmask[None, :, None, None, :])
    o_cmp = jnp.einsum('btgrn,bngd->btgrd', p_cmp.astype(cmp_v.dtype), cmp_v)
    n_slc = sel_kb.shape[2]
    slc_start = jnp.arange(n_slc) * SEL_BLOCK
    overlap = ((cmp_start[:, None] <= slc_start[None, :] + SEL_BLOCK - 1)
               & (cmp_start[:, None] + CMP_LEN - 1 >= slc_start[None, :])).astype(jnp.float32)
    imp = jnp.einsum('btgrn,nj->btgj', p_cmp, overlap)
    blk = jnp.arange(n_slc)[None, :]
    cur = (qpos // SEL_BLOCK)[:, None]
    elig = slc_start[None, :] <= qpos[:, None]
    forced = (blk == 0) | (blk == cur) | (blk == cur - 1)
    score = jnp.where(forced[None, :, None, :], FORCE_SCORE,
                      jnp.where(elig[None, :, None, :], imp, -FORCE_SCORE))
    _, idx = lax.top_k(score, min(TOPK, n_slc))
    valid = jnp.take_along_axis(jnp.broadcast_to(elig[None, :, None, :], score.shape), idx, axis=-1)
    o_sel = selected_attention(qg, idx, valid, qpos, sel_kb, sel_vb)
    o_win = window_attention(qg, win_k, win_v, win_kpos, qpos)
    g = jax.nn.sigmoid(gate.astype(jnp.float32)).astype(q.dtype).reshape(B, T, N_BRANCH, N_KV, GROUP)[..., None]
    o = g[:, :, 0] * o_cmp + g[:, :, 1] * o_sel + g[:, :, 2] * o_win
    return o.reshape(B, T, N_HEADS * HEAD_DIM)


def layer_tail(x, l, y_tok, qm, mem_k, mem_v, ffn_prev, ln_g, ln_b, w_o, w_up, ffn_conv_w, w_down):
    y = jnp.concatenate([y_tok, mem_attention(qm, mem_k, mem_v)], axis=-1) @ w_o[l]
    x = layer_norm(ALPHA * x + y, ln_g[l, 0], ln_b[l, 0])
    up = x @ w_up[l]
    z, new_prev = causal_conv(up[..., :D_FF], ffn_conv_w[l], ffn_prev)
    f = (jax.nn.silu(z) * up[..., D_FF:]) @ w_down[l]
    x = layer_norm(ALPHA * x + f, ln_g[l, 1], ln_b[l, 1])
    return x, new_prev


def a_layers(x, mem_k, mem_v, conv_prev, ffn_prev, ln_g, ln_b, w_in_a, conv_a_w, w_o, w_up, ffn_conv_w, w_down):
    conv_states, ffn_states = [], []
    for l in range(N_A):
        p = x @ w_in_a[l]
        u, b_gate, c_gate, qm = (p[..., :D_A], p[..., D_A:2 * D_A], p[..., 2 * D_A:3 * D_A], p[..., 3 * D_A:])
        c, cs = causal_conv(c_gate * u, conv_a_w[l], conv_prev[l])
        x, fs = layer_tail(x, l, b_gate * c, qm, mem_k[l], mem_v[l], ffn_prev[l],
                           ln_g, ln_b, w_o, w_up, ffn_conv_w, w_down)
        conv_states.append(cs)
        ffn_states.append(fs)
    return x, conv_states, ffn_states


def b_layers(x, qpos, keys, mem_k, mem_v, ffn_prev, ln_g, ln_b, w_in_b, w_o, w_up, ffn_conv_w, w_down):
    hq = N_HEADS * HEAD_DIM
    hg = hq + N_BRANCH * N_HEADS
    ffn_states = []
    for j in range(N_B):
        l = N_A + j
        p = x @ w_in_b[j]
        y_tok = nsa_mixer(p[..., :hq], p[..., hq:hg], *keys, qpos)
        x, fs = layer_tail(x, l, y_tok, p[..., hg:], mem_k[l], mem_v[l], ffn_prev[l],
                           ln_g, ln_b, w_o, w_up, ffn_conv_w, w_down)
        ffn_states.append(fs)
    return x, ffn_states


def setup_inputs(seed: int = 0) -> dict:
    key = jax.random.key(seed)
    ks = jax.random.split(key, 24)
    f32 = jnp.float32

    def nrm(k, shape, scale=1.0):
        return jax.random.normal(k, shape, f32) * scale

    n_pages = PAST_LEN // PAGE_SIZE
    n_pool = (5 * DEC_BATCH * n_pages + 3) // 4
    wb = min(WINDOW, PAST_LEN)
    page_table = jax.random.permutation(ks[7], n_pool)[:DEC_BATCH * n_pages].reshape(DEC_BATCH, n_pages).astype(jnp.int32)
    d_in_a = 3 * D_A + MEM_W
    d_in_b = N_HEADS * HEAD_DIM + N_BRANCH * N_HEADS + MEM_W
    return {
        'x_prompt': nrm(ks[0], (BATCH, SEQ, D_MODEL)),
        'x_sample': nrm(ks[1], (DEC_BATCH, DEC_SEQ, D_MODEL)),
        'cache_kv': nrm(ks[2], (n_pool, PAGE_SIZE, N_KV_KINDS, N_KV, HEAD_DIM)),
        'cache_win': nrm(ks[3], (DEC_BATCH, wb, 2, N_KV, HEAD_DIM)),
        'cache_mem': nrm(ks[4], (DEPTH, DEC_BATCH, N_MEM, 2, MEM_HEADS, HEAD_DIM)),
        'state_conv_mix': nrm(ks[5], (N_A, DEC_BATCH, CONV_W - 1, D_A)),
        'state_conv_ffn': nrm(ks[6], (DEPTH, DEC_BATCH, CONV_W - 1, D_FF)),
        'page_table': page_table,
        'mem_prompt': nrm(ks[8], (BATCH, N_MEM, D_MODEL)),
        'ln_g': 1.0 + nrm(ks[9], (DEPTH, 2, D_MODEL), 0.02),
        'ln_b': nrm(ks[10], (DEPTH, 2, D_MODEL), 0.02),
        'w_in_a': nrm(ks[11], (N_A, D_MODEL, d_in_a), D_MODEL ** -0.5),
        'conv_a_w': nrm(ks[12], (N_A, CONV_W, D_A), CONV_W ** -0.5),
        'w_in_b': nrm(ks[13], (N_B, D_MODEL, d_in_b), D_MODEL ** -0.5),
        'w_o': nrm(ks[14], (DEPTH, D_MIX, D_MODEL), D_MIX ** -0.5 * BETA),
        'w_mem_kv': nrm(ks[15], (DEPTH, D_MODEL, 2 * MEM_W), D_MODEL ** -0.5),
        'w_up': nrm(ks[16], (DEPTH, D_MODEL, 2 * D_FF), D_MODEL ** -0.5),
        'ffn_conv_w': nrm(ks[17], (DEPTH, CONV_W, D_FF), CONV_W ** -0.5),
        'w_down': nrm(ks[18], (DEPTH, D_FF, D_MODEL), D_FF ** -0.5 * BETA),
        'w_kv_shared': nrm(ks[19], (D_MODEL, 6 * N_KV * HEAD_DIM), D_MODEL ** -0.5),
        'cmp_pos': nrm(ks[20], (2, CMP_LEN, HEAD_DIM), 0.1),
        'w_cmp': nrm(ks[21], (2, CMP_LEN, HEAD_DIM, HEAD_DIM), (CMP_LEN * HEAD_DIM) ** -0.5),
    }


def reference(x_prompt, x_sample, cache_kv, cache_win, cache_mem, state_conv_mix, state_conv_ffn, page_table,
              mem_prompt, ln_g, ln_b, w_in_a, conv_a_w, w_in_b, w_o, w_mem_kv, w_up, ffn_conv_w, w_down,
              w_kv_shared, cmp_pos, w_cmp):
    B, T = x_prompt.shape[:2]
    mem_kv = jnp.einsum('bmd,lde->lbme', mem_prompt, w_mem_kv).reshape(DEPTH, B, N_MEM, 2, MEM_HEADS, HEAD_DIM)
    zeros_mix = jnp.zeros((N_A, B, CONV_W - 1, D_A), x_prompt.dtype)
    zeros_ffn = jnp.zeros((DEPTH, B, CONV_W - 1, D_FF), x_prompt.dtype)
    h_p, conv_p, ffn_pa = a_layers(x_prompt, mem_kv[:, :, :, 0], mem_kv[:, :, :, 1], zeros_mix, zeros_ffn,
                                   ln_g, ln_b, w_in_a, conv_a_w, w_o, w_up, ffn_conv_w, w_down)
    kv_p = (h_p @ w_kv_shared).reshape(B, T, 6, N_KV, HEAD_DIM)
    keys_p = key_side(kv_p[:, :, 0], kv_p[:, :, 1], kv_p[:, :, 2], kv_p[:, :, 3], cmp_pos, w_cmp) \
        + prompt_window_bands(kv_p[:, :, 4], kv_p[:, :, 5])
    y_prompt, ffn_pb = b_layers(h_p, jnp.arange(T), keys_p, mem_kv[:, :, :, 0], mem_kv[:, :, :, 1], zeros_ffn,
                                ln_g, ln_b, w_in_b, w_o, w_up, ffn_conv_w, w_down)
    DB, DS = x_sample.shape[:2]
    past_len = page_table.shape[1] * cache_kv.shape[1]
    wb = cache_win.shape[1]
    qpos_s = past_len + jnp.arange(DS)
    h_s, conv_s, ffn_sa = a_layers(x_sample, cache_mem[:, :, :, 0], cache_mem[:, :, :, 1], state_conv_mix,
                                   state_conv_ffn, ln_g, ln_b, w_in_a, conv_a_w, w_o, w_up, ffn_conv_w, w_down)
    kv_s = (h_s @ w_kv_shared).reshape(DB, DS, 6, N_KV, HEAD_DIM)
    rows = [jnp.concatenate([cache_kv[page_table, :, i].reshape(DB, past_len, N_KV, HEAD_DIM), kv_s[:, :, i]], axis=1)
            for i in range(N_KV_KINDS)]
    win = jnp.concatenate([cache_win, kv_s[:, :, 4:]], axis=1)
    win_kpos = (past_len - wb + jnp.arange(wb + DS))[None, :]
    keys_s = key_side(rows[0], rows[1], rows[2], rows[3], cmp_pos, w_cmp) \
        + (win[:, None, :, 0], win[:, None, :, 1], win_kpos)
    y_sample, ffn_sb = b_layers(h_s, qpos_s, keys_s, cache_mem[:, :, :, 0], cache_mem[:, :, :, 1], state_conv_ffn,
                                ln_g, ln_b, w_in_b, w_o, w_up, ffn_conv_w, w_down)
    return (y_prompt, y_sample,
            kv_p[:, :, :N_KV_KINDS], kv_p[:, max(T - WINDOW, 0):, 4:], mem_kv, jnp.stack(conv_p),
            jnp.stack(ffn_pa + ffn_pb),
            kv_s[:, :, :N_KV_KINDS], win[:, -wb:], jnp.stack(conv_s), jnp.stack(ffn_sa + ffn_sb))
```

```python
import functools

import jax
import jax.numpy as jnp
from jax import lax
from jax.experimental import pallas as pl
from jax.experimental.pallas import tpu as pltpu

F32 = jnp.float32
BF16 = jnp.bfloat16

D_MODEL = 2048
DEPTH = 4
HEAD_DIM = 128
N_A = 2
D_A = 1536
N_HEADS = 12
N_KV = 4
GROUP = 3
MEM_W = 512
N_MEM = 256
D_FF = 5504
D_FF_PAD = 5632
CMP_STRIDE = 16
CMP_LEN = 32
SEL_BLOCK = 64
TOPK = 16
WINDOW = 512
PAGE = 128
ALPHA = (2 * DEPTH) ** 0.25
LN_EPS = 1e-5
NEG = -1e30
FORCE = 1e9
SCALE = HEAD_DIM ** -0.5

VMEM_LIMIT = 56 * 1024 * 1024


def _cp(sem):
    return pltpu.CompilerParams(dimension_semantics=sem, vmem_limit_bytes=VMEM_LIMIT)


def _dot(a, b):
    return jnp.dot(a, b, preferred_element_type=F32)


def _dot_nt(a, b):
    return lax.dot_general(a, b, (((1,), (1,)), ((), ())), preferred_element_type=F32)


def _layer_norm(v, g, b):
    mu = jnp.mean(v, axis=-1, keepdims=True)
    d = v - mu
    var = jnp.mean(d * d, axis=-1, keepdims=True)
    return d * lax.rsqrt(var + LN_EPS) * g + b


def _sigmoid(x):
    return 1.0 / (1.0 + jnp.exp(-x))


def _mm_kernel(x_ref, w_ref, *rest, n_out):
    o_refs = rest[:n_out]
    xb_ref = rest[n_out]

    @pl.when(pl.program_id(1) == 0)
    def _():
        xb_ref[...] = x_ref[...].astype(BF16)

    r = _dot(xb_ref[...], w_ref[...])
    for o in o_refs:
        o[...] = r.astype(o.dtype)


def _mm(x, w, *, tm, tn, out_dtypes=(F32,)):
    m, k = x.shape
    n = w.shape[1]
    outs = pl.pallas_call(
        functools.partial(_mm_kernel, n_out=len(out_dtypes)),
        grid=(m // tm, n // tn),
        in_specs=[pl.BlockSpec((tm, k), lambda i, j: (i, 0)),
                  pl.BlockSpec((k, tn), lambda i, j: (0, j))],
        out_specs=[pl.BlockSpec((tm, tn), lambda i, j: (i, j)) for _ in out_dtypes],
        out_shape=[jax.ShapeDtypeStruct((m, n), d) for d in out_dtypes],
        scratch_shapes=[pltpu.VMEM((tm, k), BF16)],
        compiler_params=_cp(("parallel", "arbitrary")),
    )(x, w)
    return outs if len(out_dtypes) > 1 else outs[0]


def _conv_rows(v, cw_ref, tail8):
    row = lax.broadcasted_iota(jnp.int32, v.shape, 0)
    t0 = tail8[6:7, :]
    t1 = tail8[7:8, :]
    s1 = jnp.where(row == 0, t1, pltpu.roll(v, 1, 0))
    s2 = jnp.where(row == 0, t0, jnp.where(row == 1, t1, pltpu.roll(v, 2, 0)))
    return cw_ref[0:1, :] * s2 + cw_ref[1:2, :] * s1 + cw_ref[2:3, :] * v


def _amix_seq_kernel(x_ref, wu_ref, wb_ref, wc_ref, cw_ref, prev_ref, y_ref, st_ref,
                     xb_ref, carry_ref, *, tpb):
    i = pl.program_id(0)
    j = pl.program_id(1)

    @pl.when(j == 0)
    def _():
        xb_ref[...] = x_ref[...].astype(BF16)

    xb = xb_ref[...]
    cu = _dot(xb, wc_ref[...]) * _dot(xb, wu_ref[...])
    tail8 = jnp.where((i % tpb) == 0, prev_ref[0], carry_ref[j])
    conv = _conv_rows(cu, cw_ref, tail8)
    y_ref[...] = (_dot(xb, wb_ref[...]) * conv).astype(y_ref.dtype)
    last8 = cu[cu.shape[0] - 8:, :]
    carry_ref[j] = last8
    st_ref[0] = last8


def _amix_seq(x, w_in, cw, prev8, *, tm, tn, tpb):
    m = x.shape[0]
    nj = D_A // tn
    return pl.pallas_call(
        functools.partial(_amix_seq_kernel, tpb=tpb),
        grid=(m // tm, nj),
        in_specs=[pl.BlockSpec((tm, D_MODEL), lambda i, j: (i, 0)),
                  pl.BlockSpec((D_MODEL, tn), lambda i, j: (0, j)),
                  pl.BlockSpec((D_MODEL, tn), lambda i, j: (0, nj + j)),
                  pl.BlockSpec((D_MODEL, tn), lambda i, j: (0, 2 * nj + j)),
                  pl.BlockSpec((3, tn), lambda i, j: (0, j)),
                  pl.BlockSpec((1, 8, tn), lambda i, j: (i // tpb, 0, j))],
        out_specs=[pl.BlockSpec((tm, tn), lambda i, j: (i, j)),
                   pl.BlockSpec((1, 8, tn), lambda i, j: (i, 0, j))],
        out_shape=[jax.ShapeDtypeStruct((m, D_A), BF16),
                   jax.ShapeDtypeStruct((m // tm, 8, D_A), F32)],
        scratch_shapes=[pltpu.VMEM((tm, D_MODEL), BF16),
                        pltpu.VMEM((nj, 8, tn), F32)],
        compiler_params=_cp(("arbitrary", "arbitrary")),
    )(x, w_in, w_in, w_in, cw, prev8)


def _amix_tok_kernel(x_ref, wu_ref, wb_ref, wc_ref, cw_ref, p0_ref, p1_ref, y_ref, cu_ref):
    xb = x_ref[...].astype(BF16)
    cu = _dot(xb, wc_ref[...]) * _dot(xb, wu_ref[...])
    conv = cw_ref[0:1, :] * p0_ref[...] + cw_ref[1:2, :] * p1_ref[...] + cw_ref[2:3, :] * cu
    y_ref[...] = (_dot(xb, wb_ref[...]) * conv).astype(y_ref.dtype)
    cu_ref[...] = cu


def _amix_tok(x, w_in, cw, p0, p1, *, tn):
    m = x.shape[0]
    nj = D_A // tn
    return pl.pallas_call(
        _amix_tok_kernel,
        grid=(nj,),
        in_specs=[pl.BlockSpec((m, D_MODEL), lambda j: (0, 0)),
                  pl.BlockSpec((D_MODEL, tn), lambda j: (0, j)),
                  pl.BlockSpec((D_MODEL, tn), lambda j: (0, nj + j)),
                  pl.BlockSpec((D_MODEL, tn), lambda j: (0, 2 * nj + j)),
                  pl.BlockSpec((3, tn), lambda j: (0, j)),
                  pl.BlockSpec((m, tn), lambda j: (0, j)),
                  pl.BlockSpec((m, tn), lambda j: (0, j))],
        out_specs=[pl.BlockSpec((m, tn), lambda j: (0, j)),
                   pl.BlockSpec((m, tn), lambda j: (0, j))],
        out_shape=[jax.ShapeDtypeStruct((m, D_A), BF16),
                   jax.ShapeDtypeStruct((m, D_A), F32)],
        compiler_params=_cp(("parallel",)),
    )(x, w_in, w_in, w_in, cw, p0, p1)


def _memattn_kernel(x_ref, wq_ref, mk_ref, mv_ref, o_ref):
    xb = x_ref[0].astype(BF16)
    qm = _dot(xb, wq_ref[...]) * SCALE
    for h in range(4):
        sl = slice(h * HEAD_DIM, (h + 1) * HEAD_DIM)
        q = qm[:, sl].astype(BF16)
        s = _dot_nt(q, mk_ref[0, :, sl].astype(BF16))
        e = jnp.exp(s - jnp.max(s, axis=-1, keepdims=True))
        p = e / jnp.sum(e, axis=-1, keepdims=True)
        o_ref[0, :, sl] = _dot(p.astype(BF16), mv_ref[0, :, sl].astype(BF16)).astype(o_ref.dtype)


def _memattn(x3, wq, memkv, *, tt, out_dtype):
    nb, t, _ = x3.shape
    return pl.pallas_call(
        _memattn_kernel,
        grid=(nb, t // tt),
        in_specs=[pl.BlockSpec((1, tt, D_MODEL), lambda b, i: (b, i, 0)),
                  pl.BlockSpec((D_MODEL, MEM_W), lambda b, i: (0, 0)),
                  pl.BlockSpec((1, N_MEM, MEM_W), lambda b, i: (b, 0, 0)),
                  pl.BlockSpec((1, N_MEM, MEM_W), lambda b, i: (b, 0, 1))],
        out_specs=pl.BlockSpec((1, tt, MEM_W), lambda b, i: (b, i, 0)),
        out_shape=jax.ShapeDtypeStruct((nb, t, MEM_W), out_dtype),
        compiler_params=_cp(("parallel", "parallel")),
    )(x3, wq, memkv, memkv)


def _oproj_kernel(x_ref, y_ref, m_ref, wy_ref, wm_ref, g_ref, b_ref, o_ref):
    acc = _dot(y_ref[...].astype(BF16), wy_ref[...]) + _dot(m_ref[...].astype(BF16), wm_ref[...])
    o_ref[...] = _layer_norm(ALPHA * x_ref[...] + acc, g_ref[...], b_ref[...])


def _oproj(x, y, mo, wo, g, b, *, tm):
    m = x.shape[0]
    return pl.pallas_call(
        _oproj_kernel,
        grid=(m // tm,),
        in_specs=[pl.BlockSpec((tm, D_MODEL), lambda i: (i, 0)),
                  pl.BlockSpec((tm, D_A), lambda i: (i, 0)),
                  pl.BlockSpec((tm, MEM_W), lambda i: (i, 0)),
                  pl.BlockSpec((D_A, D_MODEL), lambda i: (0, 0)),
                  pl.BlockSpec((MEM_W, D_MODEL), lambda i: (D_A // MEM_W, 0)),
                  pl.BlockSpec((1, D_MODEL), lambda i: (0, 0)),
                  pl.BlockSpec((1, D_MODEL), lambda i: (0, 0))],
        out_specs=pl.BlockSpec((tm, D_MODEL), lambda i: (i, 0)),
        out_shape=jax.ShapeDtypeStruct((m, D_MODEL), F32),
        compiler_params=_cp(("parallel",)),
    )(x, y, mo, wo, wo, g, b)


def _ffn_seq_kernel(x_ref, wz_ref, wg_ref, cw_ref, wd_ref, prev_ref, g_ref, b_ref,
                    o_ref, st_ref, xb_ref, acc_ref, carry_ref, *, tpb):
    i = pl.program_id(0)
    j = pl.program_id(1)

    @pl.when(j == 0)
    def _():
        xb_ref[...] = x_ref[...].astype(BF16)
        acc_ref[...] = jnp.zeros_like(acc_ref)

    xb = xb_ref[...]
    uz = _dot(xb, wz_ref[...])
    tail8 = jnp.where((i % tpb) == 0, prev_ref[0], carry_ref[j])
    z = _conv_rows(uz, cw_ref, tail8)
    h = z * _sigmoid(z) * _dot(xb, wg_ref[...])
    acc_ref[...] += _dot(h.astype(BF16), wd_ref[...])
    last8 = uz[uz.shape[0] - 8:, :]
    carry_ref[j] = last8
    st_ref[0] = last8

    @pl.when(j == pl.num_programs(1) - 1)
    def _():
        o_ref[...] = _layer_norm(ALPHA * x_ref[...] + acc_ref[...], g_ref[...], b_ref[...])


def _ffn_seq(x, wup, cw, wd, prev8, g, b, *, tm, tf, tpb):
    m = x.shape[0]
    nj = D_FF_PAD // tf
    return pl.pallas_call(
        functools.partial(_ffn_seq_kernel, tpb=tpb),
        grid=(m // tm, nj),
        in_specs=[pl.BlockSpec((tm, D_MODEL), lambda i, j: (i, 0)),
                  pl.BlockSpec((D_MODEL, tf), lambda i, j: (0, j)),
                  pl.BlockSpec((D_MODEL, tf), lambda i, j: (0, nj + j)),
                  pl.BlockSpec((3, tf), lambda i, j: (0, j)),
                  pl.BlockSpec((tf, D_MODEL), lambda i, j: (j, 0)),
                  pl.BlockSpec((1, 8, tf), lambda i, j: (i // tpb, 0, j)),
                  pl.BlockSpec((1, D_MODEL), lambda i, j: (0, 0)),
                  pl.BlockSpec((1, D_MODEL), lambda i, j: (0, 0))],
        out_specs=[pl.BlockSpec((tm, D_MODEL), lambda i, j: (i, 0)),
                   pl.BlockSpec((1, 8, tf), lambda i, j: (i, 0, j))],
        out_shape=[jax.ShapeDtypeStruct((m, D_MODEL), F32),
                   jax.ShapeDtypeStruct((m // tm, 8, D_FF_PAD), F32)],
        scratch_shapes=[pltpu.VMEM((tm, D_MODEL), BF16),
                        pltpu.VMEM((tm, D_MODEL), F32),
                        pltpu.VMEM((nj, 8, tf), F32)],
        compiler_params=_cp(("arbitrary", "arbitrary")),
    )(x, wup, wup, cw, wd, prev8, g, b)


def _ffn_tok_kernel(x_ref, wz_ref, wg_ref, cw_ref, wd_ref, p0_ref, p1_ref, g_ref, b_ref,
                    o_ref, uz_ref, acc_ref):
    j = pl.program_id(0)

    @pl.when(j == 0)
    def _():
        acc_ref[...] = jnp.zeros_like(acc_ref)

    xb = x_ref[...].astype(BF16)
    uz = _dot(xb, wz_ref[...])
    z = cw_ref[0:1, :] * p0_ref[...] + cw_ref[1:2, :] * p1_ref[...] + cw_ref[2:3, :] * uz
    h = z * _sigmoid(z) * _dot(xb, wg_ref[...])
    acc_ref[...] += _dot(h.astype(BF16), wd_ref[...])
    uz_ref[...] = uz

    @pl.when(j == pl.num_programs(0) - 1)
    def _():
        o_ref[...] = _layer_norm(ALPHA * x_ref[...] + acc_ref[...], g_ref[...], b_ref[...])


def _ffn_tok(x, wup, cw, wd, p0, p1, g, b, *, tf):
    m = x.shape[0]
    nj = D_FF_PAD // tf
    return pl.pallas_call(
        _ffn_tok_kernel,
        grid=(nj,),
        in_specs=[pl.BlockSpec((m, D_MODEL), lambda j: (0, 0)),
                  pl.BlockSpec((D_MODEL, tf), lambda j: (0, j)),
                  pl.BlockSpec((D_MODEL, tf), lambda j: (0, nj + j)),
                  pl.BlockSpec((3, tf), lambda j: (0, j)),
                  pl.BlockSpec((tf, D_MODEL), lambda j: (j, 0)),
                  pl.BlockSpec((m, tf), lambda j: (0, j)),
                  pl.BlockSpec((m, tf), lambda j: (0, j)),
                  pl.BlockSpec((1, D_MODEL), lambda j: (0, 0)),
                  pl.BlockSpec((1, D_MODEL), lambda j: (0, 0))],
        out_specs=[pl.BlockSpec((m, D_MODEL), lambda j: (0, 0)),
                   pl.BlockSpec((m, tf), lambda j: (0, j))],
        out_shape=[jax.ShapeDtypeStruct((m, D_MODEL), F32),
                   jax.ShapeDtypeStruct((m, D_FF_PAD), F32)],
        scratch_shapes=[pltpu.VMEM((m, D_MODEL), F32)],
        compiler_params=_cp(("arbitrary",)),
    )(x, wup, wup, cw, wd, p0, p1, g, b)


def _compress_kernel(pt_ref, *refs, npg):
    del pt_ref
    page_refs = refs[:N_KV * npg]
    w_ref, o_ref, x2_ref = refs[N_KV * npg:]
    nh = npg * 8
    for g in range(N_KV):
        for pg in range(npg):
            for s in range(CMP_STRIDE):
                x2_ref[g * nh + pg * 8:g * nh + pg * 8 + 8, s * 128:(s + 1) * 128] = (
                    page_refs[g * npg + pg][0, pl.ds(s, 8, stride=CMP_STRIDE), :])
    r = _dot(x2_ref[...].astype(BF16), w_ref[0])
    for g in range(N_KV):
        o_ref[0, :, g * 128:(g + 1) * 128] = r[g * nh:(g + 1) * nh, 0:128]
        o_ref[0, :, 512 + g * 128:512 + (g + 1) * 128] = r[g * nh:(g + 1) * nh, 128:256]


def _compress(pages, pt, w2, *, npg):
    nb, npages = pt.shape
    nh = npg * 8
    page_specs = [pl.BlockSpec((1, PAGE, 128), lambda b, s, c, pt, g=g, k=k: (pt[b, s * npg + k], 0, c * N_KV + g))
                  for g in range(N_KV) for k in range(npg)]
    return pl.pallas_call(
        functools.partial(_compress_kernel, npg=npg),
        grid_spec=pltpu.PrefetchScalarGridSpec(
            num_scalar_prefetch=1,
            grid=(nb, npages // npg, 2),
            in_specs=page_specs + [pl.BlockSpec((1, 2048, 256), lambda b, s, c, pt: (c, 0, 0))],
            out_specs=pl.BlockSpec((1, nh, 1024), lambda b, s, c, pt: (b, s, c)),
            scratch_shapes=[pltpu.VMEM((N_KV * nh, 2048), F32)]),
        out_shape=jax.ShapeDtypeStruct((nb, npages * 8, 2048), F32),
        compiler_params=_cp(("parallel", "parallel", "parallel")),
    )(pt, *([pages] * (N_KV * npg)), w2)


def _posterm_kernel(pos_ref, w_ref, o_ref):
    o_ref[...] = jnp.zeros_like(o_ref)
    for c in range(2):
        x = pos_ref[c]
        w = w_ref[c]
        xh = x.astype(BF16)
        xl = (x - xh.astype(F32)).astype(BF16)
        wh = w.astype(BF16)
        wl = (w - wh.astype(F32)).astype(BF16)
        r = _dot(xh, wh) + _dot(xh, wl) + _dot(xl, wh)
        o_ref[c:c + 1, :] = r[0:1, :]


def _posterm(pos8, w2):
    return pl.pallas_call(
        _posterm_kernel,
        out_shape=jax.ShapeDtypeStruct((8, HEAD_DIM), F32),
        compiler_params=pltpu.CompilerParams(vmem_limit_bytes=VMEM_LIMIT),
    )(pos8, w2)


def _nsa_kernel(q_ref, gate_ref, ksel_ref, vsel_ref, kwin_ref, vwin_ref,
                ak_ref, bk_ref, av_ref, bv_ref, pos_ref, ovt_ref, e_ref,
                y_ref, kcat_ref, cmpk_ref, cmpv_ref, *, tq, ck):
    qi = pl.program_id(2)
    t0 = qi * tq
    n_half = ak_ref.shape[1]

    @pl.when(qi == 0)
    def _():
        kcat_ref[:, 0:HEAD_DIM] = ksel_ref[...]
        kcat_ref[:, HEAD_DIM:2 * HEAD_DIM] = e_ref[...]
        cmpk_ref[...] = (ak_ref[0] + pltpu.roll(bk_ref[0], n_half - 1, 0) + pos_ref[0:1, :]).astype(BF16)
        cmpv_ref[...] = (av_ref[0] + pltpu.roll(bv_ref[0], n_half - 1, 0) + pos_ref[1:2, :]).astype(BF16)

    q = q_ref[...] * SCALE
    qs = jnp.concatenate([q[:, r * HEAD_DIM:(r + 1) * HEAD_DIM] for r in range(GROUP)], axis=0).astype(BF16)
    tpos = t0 + lax.broadcasted_iota(jnp.int32, (tq, 1), 0)
    tpos3 = jnp.concatenate([tpos] * GROUP, axis=0)

    n_l = lax.broadcasted_iota(jnp.int32, (1, n_half), 1)
    cm = (n_l * CMP_STRIDE + (CMP_LEN - 1) <= tpos3) & (n_l < n_half - 1)
    s = jnp.where(cm, _dot_nt(qs, cmpk_ref[...]), NEG)
    e = jnp.where(cm, jnp.exp(s - jnp.max(s, axis=-1, keepdims=True)), 0.0)
    l = jnp.sum(e, axis=-1, keepdims=True)
    p = e / jnp.where(l > 0.0, l, 1.0)
    o_cmp = _dot(p.astype(BF16), cmpv_ref[...])

    ps = p[0:tq] + p[tq:2 * tq] + p[2 * tq:3 * tq]
    ph = ps.astype(BF16)
    plo = (ps - ph.astype(F32)).astype(BF16)
    imp = (_dot_nt(ovt_ref[...], ph) + _dot_nt(ovt_ref[...], plo))[0:32, :]
    jj = lax.broadcasted_iota(jnp.int32, (32, tq), 0)
    tl = t0 + lax.broadcasted_iota(jnp.int32, (32, tq), 1)
    cur = lax.shift_right_logical(tl, 6)
    forced = (jj == 0) | (jj == cur) | (jj == cur - 1)
    elig = jj * SEL_BLOCK <= tl
    score = jnp.where(forced, FORCE, jnp.where(elig, imp, -FORCE))
    sel = jnp.zeros((32, tq), F32)
    for k in range(32):
        row = score[k:k + 1, :]
        beats = (score > row) | ((score == row) & (jj < k))
        cnt = jnp.sum(beats.astype(F32), axis=0, keepdims=True)
        sel = jnp.where(jj == k, (cnt < float(TOPK)).astype(F32), sel)
    bias_t = jnp.concatenate([jnp.where(sel > 0.0, 0.0, -FORCE), jnp.zeros((96, tq), F32)], axis=0)
    bias = bias_t.T.astype(BF16)
    qcat = jnp.concatenate([qs, jnp.concatenate([bias] * GROUP, axis=0)], axis=1)

    def body(c, carry):
        m, l, acc = carry
        k0 = pl.multiple_of(c * ck, ck)
        sc = _dot_nt(qcat, kcat_ref[pl.ds(k0, ck), :])
        kpos = k0 + lax.broadcasted_iota(jnp.int32, (1, ck), 1)
        sc = jnp.where(kpos <= tpos3, sc, NEG)
        mn = jnp.maximum(m, jnp.max(sc, axis=-1, keepdims=True))
        a = jnp.exp(m - mn)
        pr = jnp.exp(sc - mn)
        l = a * l + jnp.sum(pr, axis=-1, keepdims=True)
        acc = a * acc + _dot(pr.astype(BF16), vsel_ref[pl.ds(k0, ck), :])
        return mn, l, acc

    nch = (t0 + tq + ck - 1) // ck
    m0 = jnp.full((GROUP * tq, 1), NEG, F32)
    l0 = jnp.zeros((GROUP * tq, 1), F32)
    a0 = jnp.zeros((GROUP * tq, HEAD_DIM), F32)
    _, l, acc = lax.fori_loop(0, nch, body, (m0, l0, a0))
    o_sel = acc / l

    wlen = WINDOW + tq
    ws = pl.multiple_of(jnp.maximum(t0 - WINDOW, 0), tq)
    kpos = ws + lax.broadcasted_iota(jnp.int32, (1, wlen), 1)
    wm = (kpos <= tpos3) & (kpos > tpos3 - WINDOW)
    s = jnp.where(wm, _dot_nt(qs, kwin_ref[pl.ds(ws, wlen), :]), NEG)
    e = jnp.exp(s - jnp.max(s, axis=-1, keepdims=True))
    p = e / jnp.sum(e, axis=-1, keepdims=True)
    o_win = _dot(p.astype(BF16), vwin_ref[pl.ds(ws, wlen), :])

    gs = _sigmoid(gate_ref[...])
    for r in range(GROUP):
        rs = slice(r * tq, (r + 1) * tq)
        o = (gs[:, r:r + 1] * o_cmp[rs] + gs[:, GROUP + r:GROUP + r + 1] * o_sel[rs]
             + gs[:, 2 * GROUP + r:2 * GROUP + r + 1] * o_win[rs])
        y_ref[:, r * HEAD_DIM:(r + 1) * HEAD_DIM] = o.astype(y_ref.dtype)


def _nsa_prompt(pb, kvb_rows, kvb_win, ab, pos8, ovt, emat, *, nb, t, tq, ck):
    nq = t // tq
    gw = GROUP * HEAD_DIM
    n_half = ab.shape[1]
    return pl.pallas_call(
        functools.partial(_nsa_kernel, tq=tq, ck=ck),
        grid=(nb, N_KV, nq),
        in_specs=[pl.BlockSpec((tq, gw), lambda b, g, i: (b * nq + i, g)),
                  pl.BlockSpec((tq, 128), lambda b, g, i: (b * nq + i, D_A // 128 + g)),
                  pl.BlockSpec((t, 128), lambda b, g, i: (b, 8 + g)),
                  pl.BlockSpec((t, 128), lambda b, g, i: (b, 12 + g)),
                  pl.BlockSpec((t, 128), lambda b, g, i: (b, g)),
                  pl.BlockSpec((t, 128), lambda b, g, i: (b, 4 + g)),
                  pl.BlockSpec((1, n_half, 128), lambda b, g, i: (b, 0, g)),
                  pl.BlockSpec((1, n_half, 128), lambda b, g, i: (b, 0, 4 + g)),
                  pl.BlockSpec((1, n_half, 128), lambda b, g, i: (b, 0, 8 + g)),
                  pl.BlockSpec((1, n_half, 128), lambda b, g, i: (b, 0, 12 + g)),
                  pl.BlockSpec((8, 128), lambda b, g, i: (0, 0)),
                  pl.BlockSpec((128, 128), lambda b, g, i: (0, 0)),
                  pl.BlockSpec((t, 128), lambda b, g, i: (0, 0))],
        out_specs=pl.BlockSpec((tq, gw), lambda b, g, i: (b * nq + i, g)),
        out_shape=jax.ShapeDtypeStruct((nb * t, D_A), BF16),
        scratch_shapes=[pltpu.VMEM((t, 2 * HEAD_DIM), BF16),
                        pltpu.VMEM((n_half, HEAD_DIM), BF16),
                        pltpu.VMEM((n_half, HEAD_DIM), BF16)],
        compiler_params=_cp(("parallel", "parallel", "arbitrary")),
    )(pb, pb, kvb_rows, kvb_rows, kvb_win, kvb_win, ab, ab, ab, ab, pos8, ovt, emat)


def _dec1_kernel(q_ref, ab_ref, pos_ref, ov_ref, cwin_ref, knew_ref, vnew_ref,
                 ocmp_ref, owin_ref, idx_ref, *, n_slc):
    n_half = ab_ref.shape[1]
    nlane = ov_ref.shape[1]
    lane = lax.broadcasted_iota(jnp.int32, (8, nlane), 1)
    row8 = lax.broadcasted_iota(jnp.int32, (8, nlane), 0)
    imp_all = jnp.zeros((8, nlane), F32)
    for g in range(N_KV):
        sl = slice(g * HEAD_DIM, (g + 1) * HEAD_DIM)
        qg = (q_ref[0, g] * SCALE).astype(BF16)
        cmpk = (ab_ref[0, :, sl] + pltpu.roll(ab_ref[0, :, 512 + g * 128:512 + (g + 1) * 128], n_half - 1, 0)
                + pos_ref[0:1, :]).astype(BF16)
        cmpv = (ab_ref[0, :, 1024 + g * 128:1024 + (g + 1) * 128]
                + pltpu.roll(ab_ref[0, :, 1536 + g * 128:1536 + (g + 1) * 128], n_half - 1, 0)
                + pos_ref[1:2, :]).astype(BF16)
        n_l = lax.broadcasted_iota(jnp.int32, (1, n_half), 1)
        cm = n_l < n_half - 1
        s = jnp.where(cm, _dot_nt(qg, cmpk), NEG)
        e = jnp.where(cm, jnp.exp(s - jnp.max(s, axis=-1, keepdims=True)), 0.0)
        p = e / jnp.sum(e, axis=-1, keepdims=True)
        ocmp_ref[0, g] = _dot(p.astype(BF16), cmpv)
        ps = p[0:1] + p[1:2] + p[2:3]
        ps8 = jnp.broadcast_to(ps, (8, n_half))
        ph = ps8.astype(BF16)
        plo = (ps8 - ph.astype(F32)).astype(BF16)
        imp = _dot(ph, ov_ref[...]) + _dot(plo, ov_ref[...])
        imp_all = jnp.where(row8 == g, imp, imp_all)

        kc = cwin_ref[0, :, sl].astype(BF16)
        vc = cwin_ref[0, :, 512 + g * 128:512 + (g + 1) * 128].astype(BF16)
        kn = knew_ref[0, g][0:1, :].astype(BF16).astype(F32)
        vn = vnew_ref[0, g][0:1, :].astype(BF16).astype(F32)
        w_l = lax.broadcasted_iota(jnp.int32, (1, kc.shape[0]), 1)
        sc = jnp.where(w_l >= 1, _dot_nt(qg, kc), NEG)
        sn = jnp.sum(qg.astype(F32) * kn, axis=-1, keepdims=True)
        m = jnp.maximum(jnp.max(sc, axis=-1, keepdims=True), sn)
        ec = jnp.exp(sc - m)
        en = jnp.exp(sn - m)
        lw = jnp.sum(ec, axis=-1, keepdims=True) + en
        owin_ref[0, g] = _dot((ec / lw).astype(BF16), vc) + (en / lw).astype(BF16).astype(F32) * vn

    cur = n_slc - 1
    forced = (lane == 0) | (lane == cur) | (lane == cur - 1)
    score = jnp.where(forced, FORCE, jnp.where(lane < n_slc, imp_all, -3.0 * FORCE))
    idx = jnp.zeros((8, 128), F32)
    lane128 = lax.broadcasted_iota(jnp.int32, (8, 128), 1)
    lane_f = lane.astype(F32)
    for it in range(TOPK):
        m = jnp.max(score, axis=-1, keepdims=True)
        ix = jnp.min(jnp.where(score == m, lane_f, 1e6), axis=-1, keepdims=True)
        idx = jnp.where(lane128 == it, ix, idx)
        score = jnp.where(lane_f == ix, -4.0 * FORCE, score)
    idx_ref[0] = idx.astype(jnp.int32)


def _dec1(q4, ab, pos8, ov, cwin, knew4, vnew4, *, n_slc):
    nb = q4.shape[0]
    n_half = ab.shape[1]
    hg = pl.BlockSpec((1, N_KV, 8, HEAD_DIM), lambda b: (b, 0, 0, 0))
    return pl.pallas_call(
        functools.partial(_dec1_kernel, n_slc=n_slc),
        grid=(nb,),
        in_specs=[hg,
                  pl.BlockSpec((1, n_half, 2048), lambda b: (b, 0, 0)),
                  pl.BlockSpec((8, 128), lambda b: (0, 0)),
                  pl.BlockSpec(ov.shape, lambda b: (0, 0)),
                  pl.BlockSpec((1, WINDOW, 1024), lambda b: (b, 0, 0)),
                  hg, hg],
        out_specs=[hg, hg, pl.BlockSpec((1, 8, 128), lambda b: (b, 0, 0))],
        out_shape=[jax.ShapeDtypeStruct((nb, N_KV, 8, HEAD_DIM), F32),
                   jax.ShapeDtypeStruct((nb, N_KV, 8, HEAD_DIM), F32),
                   jax.ShapeDtypeStruct((nb, 8, 128), jnp.int32)],
        compiler_params=_cp(("parallel",)),
    )(q4, ab, pos8, ov, cwin, knew4, vnew4)


def _dec2_kernel(pt_ref, si_ref, *refs, new_blk):
    del pt_ref
    k_refs = refs[:TOPK]
    v_refs = refs[TOPK:2 * TOPK]
    q_ref, knew_ref, vnew_ref, ocmp_ref, owin_ref, gate_ref, y_ref = refs[2 * TOPK:]
    b = pl.program_id(0)
    g = pl.program_id(1)
    qg = (q_ref[0, 0] * SCALE).astype(BF16)
    row = lax.broadcasted_iota(jnp.int32, (SEL_BLOCK, HEAD_DIM), 0)
    lane = lax.broadcasted_iota(jnp.int32, (1, TOPK * SEL_BLOCK), 1)
    kn = knew_ref[0, 0][0:1, :]
    vn = vnew_ref[0, 0][0:1, :]
    ks, vs = [], []
    masked = jnp.zeros((1, TOPK * SEL_BLOCK), jnp.bool_)
    lane_blk = lax.shift_right_logical(lane, 6)
    lane_off = lane & (SEL_BLOCK - 1)
    for kk in range(TOPK):
        is_new = si_ref[(b * N_KV + g) * TOPK + kk] == new_blk
        sub = row == jnp.where(is_new, 0, -1)
        ks.append(jnp.where(sub, kn, k_refs[kk][0]).astype(BF16))
        vs.append(jnp.where(sub, vn, v_refs[kk][0]).astype(BF16))
        masked = masked | ((lane_blk == jnp.where(is_new, kk, -1)) & (lane_off > 0))
    kall = jnp.concatenate(ks, axis=0)
    vall = jnp.concatenate(vs, axis=0)
    s = jnp.where(masked, NEG, _dot_nt(qg, kall))
    e = jnp.exp(s - jnp.max(s, axis=-1, keepdims=True))
    p = e / jnp.sum(e, axis=-1, keepdims=True)
    o_sel = _dot(p.astype(BF16), vall)
    gs = _sigmoid(gate_ref[0, 0])
    y_ref[0, 0] = gs[:, 0:1] * ocmp_ref[0, 0] + gs[:, 1:2] * o_sel + gs[:, 2:3] * owin_ref[0, 0]


def _dec2(pt, sidx, cache2, q4, knew4, vnew4, ocmp, owin, gate4, *, new_blk):
    nb = q4.shape[0]
    last = new_blk - 1

    def half_page(b, g, pt, si, kk):
        j = jnp.minimum(si[(b * N_KV + g) * TOPK + kk], last)
        return pt[b, lax.shift_right_logical(j, 1)] * 2 + (j & 1)

    k_specs = [pl.BlockSpec((1, SEL_BLOCK, 128), lambda b, g, pt, si, kk=kk: (half_page(b, g, pt, si, kk), 0, 8 + g))
               for kk in range(TOPK)]
    v_specs = [pl.BlockSpec((1, SEL_BLOCK, 128), lambda b, g, pt, si, kk=kk: (half_page(b, g, pt, si, kk), 0, 12 + g))
               for kk in range(TOPK)]
    hg = pl.BlockSpec((1, 1, 8, HEAD_DIM), lambda b, g, pt, si: (b, g, 0, 0))
    return pl.pallas_call(
        functools.partial(_dec2_kernel, new_blk=new_blk),
        grid_spec=pltpu.PrefetchScalarGridSpec(
            num_scalar_prefetch=2,
            grid=(nb, N_KV),
            in_specs=k_specs + v_specs + [hg] * 6,
            out_specs=hg),
        out_shape=jax.ShapeDtypeStruct((nb, N_KV, 8, HEAD_DIM), F32),
        compiler_params=_cp(("parallel", "parallel")),
    )(pt, sidx, *([cache2] * (2 * TOPK)), q4, knew4, vnew4, ocmp, owin, gate4)


def _pad_rows(a, rows):
    return jnp.pad(a, ((0, rows - a.shape[0]),) + ((0, 0),) * (a.ndim - 1))


def _to_hg(a2d, nb):
    g = a2d.shape[1] // (N_KV * HEAD_DIM)
    a = a2d.reshape(nb, N_KV, g, HEAD_DIM)
    return jnp.pad(a, ((0, 0), (0, 0), (0, 8 - g), (0, 0)))


def kernel(x_prompt, x_sample, cache_kv, cache_win, cache_mem, state_conv_mix, state_conv_ffn, page_table,
           mem_prompt, ln_g, ln_b, w_in_a, conv_a_w, w_in_b, w_o, w_mem_kv, w_up, ffn_conv_w, w_down,
           w_kv_shared, cmp_pos, w_cmp):
    nb, t = x_prompt.shape[:2]
    db = x_sample.shape[0]
    m_p = nb * t
    ms = 16
    tm = 512
    tpb = t // tm
    past = page_table.shape[1] * cache_kv.shape[1]

    w_in_a_b = w_in_a.astype(BF16)
    hq = N_HEADS * HEAD_DIM
    hgate = hq + 3 * N_HEADS
    wgate = w_in_b[:, :, hq:hgate].reshape(2, D_MODEL, 3, N_KV, GROUP)
    wgate = jnp.transpose(wgate, (0, 1, 3, 2, 4)).reshape(2, D_MODEL, N_KV, 3 * GROUP)
    wgate = jnp.pad(wgate, ((0, 0), (0, 0), (0, 0), (0, 128 - 3 * GROUP))).reshape(2, D_MODEL, N_KV * 128)
    w_qg_b = jnp.concatenate([w_in_b[:, :, :hq], wgate], axis=-1).astype(BF16)
    w_qm_b = jnp.concatenate([w_in_a[:, :, 3 * D_A:], w_in_b[:, :, hgate:]], axis=0).astype(BF16)
    w_o_b = w_o.astype(BF16)
    padf = D_FF_PAD - D_FF
    w_up_b = jnp.concatenate([jnp.pad(w_up[:, :, :D_FF], ((0, 0), (0, 0), (0, padf))),
                              jnp.pad(w_up[:, :, D_FF:], ((0, 0), (0, 0), (0, padf)))], axis=-1).astype(BF16)
    w_down_b = jnp.pad(w_down, ((0, 0), (0, padf), (0, 0))).astype(BF16)
    ffn_cw = jnp.pad(ffn_conv_w, ((0, 0), (0, 0), (0, padf)))
    w_kv_b = w_kv_shared.astype(BF16)
    w_mem_b = w_mem_kv.astype(BF16)
    w_cmp2 = jnp.concatenate([w_cmp[:, :CMP_STRIDE].reshape(2, 2048, 128),
                              w_cmp[:, CMP_STRIDE:].reshape(2, 2048, 128)], axis=-1).astype(BF16)
    pos8 = _posterm(jnp.broadcast_to(cmp_pos.reshape(2, 1, CMP_LEN * HEAD_DIM), (2, 8, CMP_LEN * HEAD_DIM)),
                    w_cmp.reshape(2, CMP_LEN * HEAD_DIM, HEAD_DIM))
    g2 = ln_g.reshape(DEPTH, 2, 1, D_MODEL)
    b2 = ln_b.reshape(DEPTH, 2, 1, D_MODEL)

    def overlap(n_cmp, n_lanes_j, n_rows):
        n = jnp.arange(n_rows)[:, None]
        j = jnp.arange(n_lanes_j)[None, :]
        return ((n >= 4 * j - 1) & (n <= 4 * j + 3) & (n < n_cmp)).astype(BF16)

    ovt_p = overlap(t // CMP_STRIDE - 1, 128, 128).T
    emat = (jnp.arange(t)[:, None] // SEL_BLOCK == jnp.arange(128)[None, :]).astype(BF16)
    n_half_s = past // CMP_STRIDE
    n_slc_s = past // SEL_BLOCK + 1
    ov_s = overlap(n_half_s - 1, 384, n_half_s)

    x = x_prompt.reshape(m_p, D_MODEL)
    mem_x = mem_prompt.reshape(nb * N_MEM, D_MODEL)
    mem_kv = [_mm(mem_x, w_mem_b[l], tm=512, tn=512).reshape(nb, N_MEM, 2 * MEM_W) for l in range(DEPTH)]
    zeros_mix = jnp.zeros((nb, 8, D_A), F32)
    zeros_ffn = jnp.zeros((nb, 8, D_FF_PAD), F32)
    conv_p, ffn_p = [], []

    def tail(x, l, y, memkv_l, seq_prev8):
        mo = _memattn(x.reshape(nb, t, D_MODEL), w_qm_b[l], memkv_l, tt=tm, out_dtype=BF16).reshape(m_p, MEM_W)
        x = _oproj(x, y, mo, w_o_b[l], g2[l, 0], b2[l, 0], tm=tm)
        x, st = _ffn_seq(x, w_up_b[l], ffn_cw[l], w_down_b[l], seq_prev8, g2[l, 1], b2[l, 1],
                         tm=tm, tf=512, tpb=tpb)
        return x, st

    for l in range(N_A):
        y, st = _amix_seq(x, w_in_a_b[l], conv_a_w[l], zeros_mix, tm=tm, tn=512, tpb=tpb)
        conv_p.append(st[tpb - 1::tpb, 6:8])
        x, fs = tail(x, l, y, mem_kv[l], zeros_ffn)
        ffn_p.append(fs[tpb - 1::tpb, 6:8, :D_FF])

    kv_rows, kvb_rows = _mm(x, w_kv_b[:, :2048], tm=tm, tn=512, out_dtypes=(F32, BF16))
    kv_win, kvb_win = _mm(x, w_kv_b[:, 2048:], tm=tm, tn=512, out_dtypes=(F32, BF16))
    pt_p = jnp.arange(m_p // PAGE, dtype=jnp.int32).reshape(nb, t // PAGE)
    ab_p = _compress(kv_rows.reshape(m_p // PAGE, PAGE, 2048), pt_p, w_cmp2, npg=t // PAGE)

    for j in range(DEPTH - N_A):
        l = N_A + j
        pb = _mm(x, w_qg_b[j], tm=tm, tn=512)
        y = _nsa_prompt(pb, kvb_rows, kvb_win, ab_p, pos8, ovt_p, emat, nb=nb, t=t, tq=128, ck=512)
        x, fs = tail(x, l, y, mem_kv[l], zeros_ffn)
        ffn_p.append(fs[tpb - 1::tpb, 6:8, :D_FF])
    y_prompt = x.reshape(nb, t, D_MODEL)

    xs = _pad_rows(x_sample.reshape(db, D_MODEL), ms)
    cmem = cache_mem.reshape(DEPTH, db, N_MEM, 2 * MEM_W)
    conv_s, ffn_s = [], []

    def tail_s(xs, l, y):
        xrep = jnp.broadcast_to(xs[:db, None, :], (db, 8, D_MODEL))
        mo = _pad_rows(_memattn(xrep, w_qm_b[l], cmem[l], tt=8, out_dtype=F32)[:, 0], ms)
        xs = _oproj(xs, y, mo, w_o_b[l], g2[l, 0], b2[l, 0], tm=ms)
        p0 = _pad_rows(jnp.pad(state_conv_ffn[l, :, 0], ((0, 0), (0, padf))), ms)
        p1 = _pad_rows(jnp.pad(state_conv_ffn[l, :, 1], ((0, 0), (0, padf))), ms)
        xs, uz = _ffn_tok(xs, w_up_b[l], ffn_cw[l], w_down_b[l], p0, p1, g2[l, 1], b2[l, 1], tf=512)
        ffn_s.append(jnp.stack([state_conv_ffn[l, :, 1], uz[:db, :D_FF]], axis=1))
        return xs

    for l in range(N_A):
        p0 = _pad_rows(state_conv_mix[l, :, 0], ms)
        p1 = _pad_rows(state_conv_mix[l, :, 1], ms)
        y, cu = _amix_tok(xs, w_in_a_b[l], conv_a_w[l], p0, p1, tn=512)
        conv_s.append(jnp.stack([state_conv_mix[l, :, 1], cu[:db]], axis=1))
        xs = tail_s(xs, l, y)

    kv_s = _mm(xs, w_kv_b, tm=ms, tn=512)[:db]
    cache_pages = cache_kv.reshape(cache_kv.shape[0], PAGE, 2048)
    ab_s = _compress(cache_pages, page_table, w_cmp2, npg=16)
    cache2 = cache_kv.reshape(cache_kv.shape[0] * 2, SEL_BLOCK, 2048)
    cwin = cache_win.reshape(db, WINDOW, 1024)
    ksel_new = _to_hg(kv_s[:, 1024:1536], db)
    vsel_new = _to_hg(kv_s[:, 1536:2048], db)
    kwin_new = _to_hg(kv_s[:, 2048:2560], db)
    vwin_new = _to_hg(kv_s[:, 2560:3072], db)

    for j in range(DEPTH - N_A):
        l = N_A + j
        pbs = _mm(xs, w_qg_b[j], tm=ms, tn=512)[:db]
        q4 = _to_hg(pbs[:, :hq], db)
        gate4 = pbs[:, hq:].reshape(db, N_KV, 128)[:, :, :3 * GROUP].reshape(db, N_KV, 3, GROUP)
        gate4 = jnp.pad(gate4.transpose(0, 1, 3, 2), ((0, 0), (0, 0), (0, 8 - GROUP), (0, 128 - 3)))
        ocmp, owin, idx = _dec1(q4, ab_s, pos8, ov_s, cwin, kwin_new, vwin_new, n_slc=n_slc_s)
        sidx = idx[:, :N_KV, :TOPK].reshape(db * N_KV * TOPK)
        y4 = _dec2(page_table, sidx, cache2, q4, ksel_new, vsel_new, ocmp, owin, gate4, new_blk=n_slc_s - 1)
        y = _pad_rows(y4[:, :, :GROUP].reshape(db, D_A), ms).astype(BF16)
        xs = tail_s(xs, l, y)
    y_sample = xs[:db].reshape(db, 1, D_MODEL)

    wb = cache_win.shape[1]
    win_new = kv_s[:, 2048:].reshape(db, 1, 2, N_KV, HEAD_DIM)
    return (y_prompt, y_sample,
            kv_rows.reshape(nb, t, 4, N_KV, HEAD_DIM),
            kv_win.reshape(nb, t, 2, N_KV, HEAD_DIM)[:, max(t - WINDOW, 0):],
            jnp.stack(mem_kv).reshape(DEPTH, nb, N_MEM, 2, 4, HEAD_DIM),
            jnp.stack(conv_p),
            jnp.stack(ffn_p),
            kv_s[:, :2048].reshape(db, 1, 4, N_KV, HEAD_DIM),
            jnp.concatenate([cache_win, win_new], axis=1)[:, -wb:],
            jnp.stack(conv_s),
            jnp.stack(ffn_s))
```

```python
import functools

import jax
import jax.numpy as jnp
from jax import lax
from jax.experimental import pallas as pl
from jax.experimental.pallas import tpu as pltpu

F32 = jnp.float32
BF16 = jnp.bfloat16

D_MODEL = 2048
DEPTH = 4
HEAD_DIM = 128
N_A = 2
D_A = 1536
N_HEADS = 12
N_KV = 4
GROUP = 3
MEM_W = 512
N_MEM = 256
D_FF = 5504
D_FF_PAD = 5632
CMP_STRIDE = 16
CMP_LEN = 32
SEL_BLOCK = 64
TOPK = 16
WINDOW = 512
PAGE = 128
ALPHA = (2 * DEPTH) ** 0.25
LN_EPS = 1e-5
NEG = -1e30
FORCE = 1e9
SCALE = HEAD_DIM ** -0.5

VMEM_LIMIT = 56 * 1024 * 1024


def _cp(sem):
    return pltpu.CompilerParams(dimension_semantics=sem, vmem_limit_bytes=VMEM_LIMIT)


def _dot(a, b):
    return jnp.dot(a, b, preferred_element_type=F32)


def _dot_nt(a, b):
    return lax.dot_general(a, b, (((1,), (1,)), ((), ())), preferred_element_type=F32)


def _layer_norm(v, g, b):
    mu = jnp.mean(v, axis=-1, keepdims=True)
    d = v - mu
    var = jnp.mean(d * d, axis=-1, keepdims=True)
    return d * lax.rsqrt(var + LN_EPS) * g + b


def _sigmoid(x):
    return 1.0 / (1.0 + jnp.exp(-x))


def _mm_kernel(x_ref, w_ref, *rest, n_out):
    o_refs = rest[:n_out]
    xb_ref = rest[n_out]

    @pl.when(pl.program_id(1) == 0)
    def _():
        xb_ref[...] = x_ref[...].astype(BF16)

    r = _dot(xb_ref[...], w_ref[...])
    for o in o_refs:
        o[...] = r.astype(o.dtype)


def _mm(x, w, *, tm, tn, out_dtypes=(F32,), name="mm"):
    m, k = x.shape
    n = w.shape[1]
    outs = pl.pallas_call(
        functools.partial(_mm_kernel, n_out=len(out_dtypes)),
        grid=(m // tm, n // tn),
        in_specs=[pl.BlockSpec((tm, k), lambda i, j: (i, 0)),
                  pl.BlockSpec((k, tn), lambda i, j: (0, j))],
        out_specs=[pl.BlockSpec((tm, tn), lambda i, j: (i, j)) for _ in out_dtypes],
        out_shape=[jax.ShapeDtypeStruct((m, n), d) for d in out_dtypes],
        scratch_shapes=[pltpu.VMEM((tm, k), BF16)],
        compiler_params=_cp(("parallel", "arbitrary")),
        name=name,
    )(x, w)
    return outs if len(out_dtypes) > 1 else outs[0]


def _kvrows_kernel(x_ref, w_ref, tr_ref, ob_ref, xb_ref):
    j = pl.program_id(1)

    @pl.when(j == 0)
    def _():
        xb_ref[...] = x_ref[...].astype(BF16)

    r = _dot(xb_ref[...], w_ref[...])
    ob_ref[...] = r.astype(BF16)
    tm = r.shape[0]
    for c in range(r.shape[1] // HEAD_DIM):
        tr_ref[pl.ds(j * (r.shape[1] // HEAD_DIM) + c, tm, stride=16), :] = r[:, c * HEAD_DIM:(c + 1) * HEAD_DIM]


def _kvrows(x, w, *, tm, tn):
    m, k = x.shape
    n = w.shape[1]
    assert n == 16 * HEAD_DIM
    return pl.pallas_call(
        _kvrows_kernel,
        grid=(m // tm, n // tn),
        in_specs=[pl.BlockSpec((tm, k), lambda i, j: (i, 0)),
                  pl.BlockSpec((k, tn), lambda i, j: (0, j))],
        out_specs=[pl.BlockSpec((tm * 16, HEAD_DIM), lambda i, j: (i, 0)),
                   pl.BlockSpec((tm, tn), lambda i, j: (i, j))],
        out_shape=[jax.ShapeDtypeStruct((m * 16, HEAD_DIM), F32),
                   jax.ShapeDtypeStruct((m, n), BF16)],
        scratch_shapes=[pltpu.VMEM((tm, k), BF16)],
        compiler_params=_cp(("parallel", "arbitrary")),
        name="kvrows",
    )(x, w)


def _conv_rows(v, cw_ref, tail8):
    row = lax.broadcasted_iota(jnp.int32, v.shape, 0)
    t0 = tail8[6:7, :]
    t1 = tail8[7:8, :]
    s1 = jnp.where(row == 0, t1, pltpu.roll(v, 1, 0))
    s2 = jnp.where(row == 0, t0, jnp.where(row == 1, t1, pltpu.roll(v, 2, 0)))
    return cw_ref[0:1, :] * s2 + cw_ref[1:2, :] * s1 + cw_ref[2:3, :] * v


def _amix_seq_kernel(x_ref, wu_ref, wb_ref, wc_ref, cw_ref, prev_ref, y_ref, st_ref,
                     xb_ref, carry_ref, *, tpb):
    i = pl.program_id(0)
    j = pl.program_id(1)

    @pl.when(j == 0)
    def _():
        xb_ref[...] = x_ref[...].astype(BF16)

    xb = xb_ref[...]
    cu = _dot(xb, wc_ref[...]) * _dot(xb, wu_ref[...])
    tail8 = jnp.where((i % tpb) == 0, prev_ref[0], carry_ref[j])
    conv = _conv_rows(cu, cw_ref, tail8)
    y_ref[...] = (_dot(xb, wb_ref[...]) * conv).astype(y_ref.dtype)
    last8 = cu[cu.shape[0] - 8:, :]
    carry_ref[j] = last8
    st_ref[0] = last8


def _amix_seq(x, w_in, cw, prev8, *, tm, tn, tpb):
    m = x.shape[0]
    nj = D_A // tn
    return pl.pallas_call(
        functools.partial(_amix_seq_kernel, tpb=tpb),
        grid=(m // tm, nj),
        in_specs=[pl.BlockSpec((tm, D_MODEL), lambda i, j: (i, 0)),
                  pl.BlockSpec((D_MODEL, tn), lambda i, j: (0, j)),
                  pl.BlockSpec((D_MODEL, tn), lambda i, j: (0, nj + j)),
                  pl.BlockSpec((D_MODEL, tn), lambda i, j: (0, 2 * nj + j)),
                  pl.BlockSpec((3, tn), lambda i, j: (0, j)),
                  pl.BlockSpec((1, 8, tn), lambda i, j: (i // tpb, 0, j))],
        out_specs=[pl.BlockSpec((tm, tn), lambda i, j: (i, j)),
                   pl.BlockSpec((1, 8, tn), lambda i, j: (i, 0, j))],
        out_shape=[jax.ShapeDtypeStruct((m, D_A), BF16),
                   jax.ShapeDtypeStruct((m // tm, 8, D_A), F32)],
        scratch_shapes=[pltpu.VMEM((tm, D_MODEL), BF16),
                        pltpu.VMEM((nj, 8, tn), F32)],
        compiler_params=_cp(("arbitrary", "arbitrary")),
        name="amix_seq",
    )(x, w_in, w_in, w_in, cw, prev8)


def _amix_tok_kernel(x_ref, wu_ref, wb_ref, wc_ref, cw_ref, p0_ref, p1_ref, y_ref, cu_ref):
    xb = x_ref[...].astype(BF16)
    cu = _dot(xb, wc_ref[...]) * _dot(xb, wu_ref[...])
    conv = cw_ref[0:1, :] * p0_ref[...] + cw_ref[1:2, :] * p1_ref[...] + cw_ref[2:3, :] * cu
    y_ref[...] = (_dot(xb, wb_ref[...]) * conv).astype(y_ref.dtype)
    cu_ref[...] = cu


def _amix_tok(x, w_in, cw, p0, p1, *, tn):
    m = x.shape[0]
    nj = D_A // tn
    return pl.pallas_call(
        _amix_tok_kernel,
        grid=(nj,),
        in_specs=[pl.BlockSpec((m, D_MODEL), lambda j: (0, 0)),
                  pl.BlockSpec((D_MODEL, tn), lambda j: (0, j)),
                  pl.BlockSpec((D_MODEL, tn), lambda j: (0, nj + j)),
                  pl.BlockSpec((D_MODEL, tn), lambda j: (0, 2 * nj + j)),
                  pl.BlockSpec((3, tn), lambda j: (0, j)),
                  pl.BlockSpec((m, tn), lambda j: (0, j)),
                  pl.BlockSpec((m, tn), lambda j: (0, j))],
        out_specs=[pl.BlockSpec((m, tn), lambda j: (0, j)),
                   pl.BlockSpec((m, tn), lambda j: (0, j))],
        out_shape=[jax.ShapeDtypeStruct((m, D_A), BF16),
                   jax.ShapeDtypeStruct((m, D_A), F32)],
        compiler_params=_cp(("parallel",)),
        name="amix_tok",
    )(x, w_in, w_in, w_in, cw, p0, p1)


def _memattn_kernel(x_ref, wq_ref, mk_ref, mv_ref, o_ref):
    xb = x_ref[0].astype(BF16)
    qm = _dot(xb, wq_ref[...]) * SCALE
    for h in range(4):
        sl = slice(h * HEAD_DIM, (h + 1) * HEAD_DIM)
        q = qm[:, sl].astype(BF16)
        s = _dot_nt(q, mk_ref[0, :, sl].astype(BF16))
        e = jnp.exp(s - jnp.max(s, axis=-1, keepdims=True))
        p = e / jnp.sum(e, axis=-1, keepdims=True)
        o_ref[0, :, sl] = _dot(p.astype(BF16), mv_ref[0, :, sl].astype(BF16)).astype(o_ref.dtype)


def _memattn(x3, wq, memkv, *, tt, out_dtype):
    nb, t, _ = x3.shape
    return pl.pallas_call(
        _memattn_kernel,
        grid=(nb, t // tt),
        in_specs=[pl.BlockSpec((1, tt, D_MODEL), lambda b, i: (b, i, 0)),
                  pl.BlockSpec((D_MODEL, MEM_W), lambda b, i: (0, 0)),
                  pl.BlockSpec((1, N_MEM, MEM_W), lambda b, i: (b, 0, 0)),
                  pl.BlockSpec((1, N_MEM, MEM_W), lambda b, i: (b, 0, 1))],
        out_specs=pl.BlockSpec((1, tt, MEM_W), lambda b, i: (b, i, 0)),
        out_shape=jax.ShapeDtypeStruct((nb, t, MEM_W), out_dtype),
        compiler_params=_cp(("parallel", "parallel")),
        name="memattn",
    )(x3, wq, memkv, memkv)


def _oproj_kernel(x_ref, y_ref, m_ref, wy_ref, wm_ref, g_ref, b_ref, o_ref):
    acc = _dot(y_ref[...].astype(BF16), wy_ref[...]) + _dot(m_ref[...].astype(BF16), wm_ref[...])
    o_ref[...] = _layer_norm(ALPHA * x_ref[...] + acc, g_ref[...], b_ref[...])


def _oproj(x, y, mo, wo, g, b, *, tm):
    m = x.shape[0]
    return pl.pallas_call(
        _oproj_kernel,
        grid=(m // tm,),
        in_specs=[pl.BlockSpec((tm, D_MODEL), lambda i: (i, 0)),
                  pl.BlockSpec((tm, D_A), lambda i: (i, 0)),
                  pl.BlockSpec((tm, MEM_W), lambda i: (i, 0)),
                  pl.BlockSpec((D_A, D_MODEL), lambda i: (0, 0)),
                  pl.BlockSpec((MEM_W, D_MODEL), lambda i: (D_A // MEM_W, 0)),
                  pl.BlockSpec((1, D_MODEL), lambda i: (0, 0)),
                  pl.BlockSpec((1, D_MODEL), lambda i: (0, 0))],
        out_specs=pl.BlockSpec((tm, D_MODEL), lambda i: (i, 0)),
        out_shape=jax.ShapeDtypeStruct((m, D_MODEL), F32),
        compiler_params=_cp(("parallel",)),
        name="oproj_ln",
    )(x, y, mo, wo, wo, g, b)


def _ffn_seq_kernel(x_ref, wz_ref, wg_ref, cw_ref, wd_ref, prev_ref, g_ref, b_ref,
                    o_ref, st_ref, xb_ref, acc_ref, carry_ref, *, tpb):
    i = pl.program_id(0)
    j = pl.program_id(1)

    @pl.when(j == 0)
    def _():
        xb_ref[...] = x_ref[...].astype(BF16)
        acc_ref[...] = jnp.zeros_like(acc_ref)

    xb = xb_ref[...]
    uz = _dot(xb, wz_ref[...])
    tail8 = jnp.where((i % tpb) == 0, prev_ref[0], carry_ref[j])
    z = _conv_rows(uz, cw_ref, tail8)
    h = z * _sigmoid(z) * _dot(xb, wg_ref[...])
    acc_ref[...] += _dot(h.astype(BF16), wd_ref[...])
    last8 = uz[uz.shape[0] - 8:, :]
    carry_ref[j] = last8
    st_ref[0] = last8

    @pl.when(j == pl.num_programs(1) - 1)
    def _():
        o_ref[...] = _layer_norm(ALPHA * x_ref[...] + acc_ref[...], g_ref[...], b_ref[...])


def _ffn_seq(x, wup, cw, wd, prev8, g, b, *, tm, tf, tpb):
    m = x.shape[0]
    nj = D_FF_PAD // tf
    return pl.pallas_call(
        functools.partial(_ffn_seq_kernel, tpb=tpb),
        grid=(m // tm, nj),
        in_specs=[pl.BlockSpec((tm, D_MODEL), lambda i, j: (i, 0)),
                  pl.BlockSpec((D_MODEL, tf), lambda i, j: (0, j)),
                  pl.BlockSpec((D_MODEL, tf), lambda i, j: (0, nj + j)),
                  pl.BlockSpec((3, tf), lambda i, j: (0, j)),
                  pl.BlockSpec((tf, D_MODEL), lambda i, j: (j, 0)),
                  pl.BlockSpec((1, 8, tf), lambda i, j: (i // tpb, 0, j)),
                  pl.BlockSpec((1, D_MODEL), lambda i, j: (0, 0)),
                  pl.BlockSpec((1, D_MODEL), lambda i, j: (0, 0))],
        out_specs=[pl.BlockSpec((tm, D_MODEL), lambda i, j: (i, 0)),
                   pl.BlockSpec((1, 8, tf), lambda i, j: (i, 0, j))],
        out_shape=[jax.ShapeDtypeStruct((m, D_MODEL), F32),
                   jax.ShapeDtypeStruct((m // tm, 8, D_FF_PAD), F32)],
        scratch_shapes=[pltpu.VMEM((tm, D_MODEL), BF16),
                        pltpu.VMEM((tm, D_MODEL), F32),
                        pltpu.VMEM((nj, 8, tf), F32)],
        compiler_params=_cp(("arbitrary", "arbitrary")),
        name="ffn_seq",
    )(x, wup, wup, cw, wd, prev8, g, b)


def _ffn_tok_kernel(x_ref, wz_ref, wg_ref, cw_ref, wd_ref, p0_ref, p1_ref, g_ref, b_ref,
                    o_ref, uz_ref, acc_ref):
    j = pl.program_id(0)

    @pl.when(j == 0)
    def _():
        acc_ref[...] = jnp.zeros_like(acc_ref)

    xb = x_ref[...].astype(BF16)
    uz = _dot(xb, wz_ref[...])
    z = cw_ref[0:1, :] * p0_ref[...] + cw_ref[1:2, :] * p1_ref[...] + cw_ref[2:3, :] * uz
    h = z * _sigmoid(z) * _dot(xb, wg_ref[...])
    acc_ref[...] += _dot(h.astype(BF16), wd_ref[...])
    uz_ref[...] = uz

    @pl.when(j == pl.num_programs(0) - 1)
    def _():
        o_ref[...] = _layer_norm(ALPHA * x_ref[...] + acc_ref[...], g_ref[...], b_ref[...])


def _ffn_tok(x, wup, cw, wd, p0, p1, g, b, *, tf):
    m = x.shape[0]
    nj = D_FF_PAD // tf
    return pl.pallas_call(
        _ffn_tok_kernel,
        grid=(nj,),
        in_specs=[pl.BlockSpec((m, D_MODEL), lambda j: (0, 0)),
                  pl.BlockSpec((D_MODEL, tf), lambda j: (0, j)),
                  pl.BlockSpec((D_MODEL, tf), lambda j: (0, nj + j)),
                  pl.BlockSpec((3, tf), lambda j: (0, j)),
                  pl.BlockSpec((tf, D_MODEL), lambda j: (j, 0)),
                  pl.BlockSpec((m, tf), lambda j: (0, j)),
                  pl.BlockSpec((m, tf), lambda j: (0, j)),
                  pl.BlockSpec((1, D_MODEL), lambda j: (0, 0)),
                  pl.BlockSpec((1, D_MODEL), lambda j: (0, 0))],
        out_specs=[pl.BlockSpec((m, D_MODEL), lambda j: (0, 0)),
                   pl.BlockSpec((m, tf), lambda j: (0, j))],
        out_shape=[jax.ShapeDtypeStruct((m, D_MODEL), F32),
                   jax.ShapeDtypeStruct((m, D_FF_PAD), F32)],
        scratch_shapes=[pltpu.VMEM((m, D_MODEL), F32)],
        compiler_params=_cp(("arbitrary",)),
        name="ffn_tok",
    )(x, wup, wup, cw, wd, p0, p1, g, b)


def _compress_kernel(pt_ref, *refs, npg):
    del pt_ref
    page_refs = refs[:npg]
    w_ref, o_ref, x2_ref = refs[npg:]
    nh = npg * 8
    for pg in range(npg):
        for r in range(PAGE):
            n = pg * 8 + r // CMP_STRIDE
            x2_ref[r % CMP_STRIDE, n * 8:n * 8 + 8, :] = page_refs[pg][r]
    for c in range(2):
        xs = [jnp.concatenate([x2_ref[s, pl.ds(c * N_KV + g, nh, stride=8), :] for s in range(CMP_STRIDE)], axis=1)
              for g in range(N_KV)]
        r = _dot(jnp.concatenate(xs, axis=0).astype(BF16), w_ref[c])
        for g in range(N_KV):
            o_ref[0, :, c * 1024 + g * 128:c * 1024 + (g + 1) * 128] = r[g * nh:(g + 1) * nh, 0:128]
            o_ref[0, :, c * 1024 + 512 + g * 128:c * 1024 + 512 + (g + 1) * 128] = r[g * nh:(g + 1) * nh, 128:256]


def _compress(rows3, pt, w2, *, npg):
    nb, npages = pt.shape
    nh = npg * 8
    page_specs = [pl.BlockSpec((PAGE, 8, 128), lambda b, s, pt, k=k: (pt[b, s * npg + k], 0, 0))
                  for k in range(npg)]
    return pl.pallas_call(
        functools.partial(_compress_kernel, npg=npg),
        grid_spec=pltpu.PrefetchScalarGridSpec(
            num_scalar_prefetch=1,
            grid=(nb, npages // npg),
            in_specs=page_specs + [pl.BlockSpec((2, 2048, 256), lambda b, s, pt: (0, 0, 0))],
            out_specs=pl.BlockSpec((1, nh, 2048), lambda b, s, pt: (b, s, 0)),
            scratch_shapes=[pltpu.VMEM((CMP_STRIDE, nh * 8, 128), F32)]),
        out_shape=jax.ShapeDtypeStruct((nb, npages * 8, 2048), F32),
        compiler_params=_cp(("parallel", "parallel")),
        name="compress",
    )(pt, *([rows3] * npg), w2)


def _posterm_kernel(pos_ref, w_ref, o_ref):
    o_ref[...] = jnp.zeros_like(o_ref)
    for c in range(2):
        x = pos_ref[c]
        w = w_ref[c]
        xh = x.astype(BF16)
        xl = (x - xh.astype(F32)).astype(BF16)
        wh = w.astype(BF16)
        wl = (w - wh.astype(F32)).astype(BF16)
        r = _dot(xh, wh) + _dot(xh, wl) + _dot(xl, wh)
        o_ref[c:c + 1, :] = r[0:1, :]


def _posterm(pos8, w2):
    return pl.pallas_call(
        _posterm_kernel,
        out_shape=jax.ShapeDtypeStruct((8, HEAD_DIM), F32),
        compiler_params=pltpu.CompilerParams(vmem_limit_bytes=VMEM_LIMIT),
        name="posterm",
    )(pos8, w2)


def _nsa_kernel(q_ref, gate_ref, ksel_ref, vsel_ref, kwin_ref, vwin_ref,
                ak_ref, bk_ref, av_ref, bv_ref, pos_ref, ovt_ref, e_ref,
                y_ref, kcat_ref, cmpk_ref, cmpv_ref, *, tq, ck):
    qi = pl.program_id(2)
    t0 = qi * tq
    n_half = ak_ref.shape[1]

    @pl.when(qi == 0)
    def _():
        kcat_ref[:, 0:HEAD_DIM] = ksel_ref[...]
        kcat_ref[:, HEAD_DIM:2 * HEAD_DIM] = e_ref[...]
        cmpk_ref[...] = (ak_ref[0] + pltpu.roll(bk_ref[0], n_half - 1, 0) + pos_ref[0:1, :]).astype(BF16)
        cmpv_ref[...] = (av_ref[0] + pltpu.roll(bv_ref[0], n_half - 1, 0) + pos_ref[1:2, :]).astype(BF16)

    q = q_ref[...] * SCALE
    qs = jnp.concatenate([q[:, r * HEAD_DIM:(r + 1) * HEAD_DIM] for r in range(GROUP)], axis=0).astype(BF16)
    tpos = t0 + lax.broadcasted_iota(jnp.int32, (tq, 1), 0)
    tpos3 = jnp.concatenate([tpos] * GROUP, axis=0)

    n_l = lax.broadcasted_iota(jnp.int32, (1, n_half), 1)
    cm = (n_l * CMP_STRIDE + (CMP_LEN - 1) <= tpos3) & (n_l < n_half - 1)
    s = jnp.where(cm, _dot_nt(qs, cmpk_ref[...]), NEG)
    e = jnp.where(cm, jnp.exp(s - jnp.max(s, axis=-1, keepdims=True)), 0.0)
    l = jnp.sum(e, axis=-1, keepdims=True)
    p = e / jnp.where(l > 0.0, l, 1.0)
    o_cmp = _dot(p.astype(BF16), cmpv_ref[...])

    ps = p[0:tq] + p[tq:2 * tq] + p[2 * tq:3 * tq]
    ph = ps.astype(BF16)
    plo = (ps - ph.astype(F32)).astype(BF16)
    imp = (_dot_nt(ovt_ref[...], ph) + _dot_nt(ovt_ref[...], plo))[0:32, :]
    jj = lax.broadcasted_iota(jnp.int32, (32, tq), 0)
    tl = t0 + lax.broadcasted_iota(jnp.int32, (32, tq), 1)
    cur = lax.shift_right_logical(tl, 6)
    forced = (jj == 0) | (jj == cur) | (jj == cur - 1)
    elig = jj * SEL_BLOCK <= tl
    score = jnp.where(forced, FORCE, jnp.where(elig, imp, -FORCE))
    sel = jnp.zeros((32, tq), F32)
    for k in range(32):
        row = score[k:k + 1, :]
        beats = (score > row) | ((score == row) & (jj < k))
        cnt = jnp.sum(beats.astype(F32), axis=0, keepdims=True)
        sel = jnp.where(jj == k, (cnt < float(TOPK)).astype(F32), sel)
    bias_t = jnp.concatenate([jnp.where(sel > 0.0, 0.0, -FORCE), jnp.zeros((96, tq), F32)], axis=0)
    bias = bias_t.T.astype(BF16)
    qcat = jnp.concatenate([qs, jnp.concatenate([bias] * GROUP, axis=0)], axis=1)

    def body(c, carry):
        m, l, acc = carry
        k0 = pl.multiple_of(c * ck, ck)
        sc = _dot_nt(qcat, kcat_ref[pl.ds(k0, ck), :])
        kpos = k0 + lax.broadcasted_iota(jnp.int32, (1, ck), 1)
        sc = jnp.where(kpos <= tpos3, sc, NEG)
        mn = jnp.maximum(m, jnp.max(sc, axis=-1, keepdims=True))
        a = jnp.exp(m - mn)
        pr = jnp.exp(sc - mn)
        l = a * l + jnp.sum(pr, axis=-1, keepdims=True)
        acc = a * acc + _dot(pr.astype(BF16), vsel_ref[pl.ds(k0, ck), :])
        return mn, l, acc

    nch = (t0 + tq + ck - 1) // ck
    m0 = jnp.full((GROUP * tq, 1), NEG, F32)
    l0 = jnp.zeros((GROUP * tq, 1), F32)
    a0 = jnp.zeros((GROUP * tq, HEAD_DIM), F32)
    _, l, acc = lax.fori_loop(0, nch, body, (m0, l0, a0))
    o_sel = acc / l

    wlen = WINDOW + tq
    ws = pl.multiple_of(jnp.maximum(t0 - WINDOW, 0), tq)
    kpos = ws + lax.broadcasted_iota(jnp.int32, (1, wlen), 1)
    wm = (kpos <= tpos3) & (kpos > tpos3 - WINDOW)
    s = jnp.where(wm, _dot_nt(qs, kwin_ref[pl.ds(ws, wlen), :]), NEG)
    e = jnp.exp(s - jnp.max(s, axis=-1, keepdims=True))
    p = e / jnp.sum(e, axis=-1, keepdims=True)
    o_win = _dot(p.astype(BF16), vwin_ref[pl.ds(ws, wlen), :])

    gs = _sigmoid(gate_ref[...])
    for r in range(GROUP):
        rs = slice(r * tq, (r + 1) * tq)
        o = (gs[:, r:r + 1] * o_cmp[rs] + gs[:, GROUP + r:GROUP + r + 1] * o_sel[rs]
             + gs[:, 2 * GROUP + r:2 * GROUP + r + 1] * o_win[rs])
        y_ref[:, r * HEAD_DIM:(r + 1) * HEAD_DIM] = o.astype(y_ref.dtype)


def _nsa_prompt(pb, kvb_rows, kvb_win, ab, pos8, ovt, emat, *, nb, t, tq, ck):
    nq = t // tq
    gw = GROUP * HEAD_DIM
    n_half = ab.shape[1]
    return pl.pallas_call(
        functools.partial(_nsa_kernel, tq=tq, ck=ck),
        grid=(nb, N_KV, nq),
        in_specs=[pl.BlockSpec((tq, gw), lambda b, g, i: (b * nq + i, g)),
                  pl.BlockSpec((tq, 128), lambda b, g, i: (b * nq + i, D_A // 128 + g)),
                  pl.BlockSpec((t, 128), lambda b, g, i: (b, 8 + g)),
                  pl.BlockSpec((t, 128), lambda b, g, i: (b, 12 + g)),
                  pl.BlockSpec((t, 128), lambda b, g, i: (b, g)),
                  pl.BlockSpec((t, 128), lambda b, g, i: (b, 4 + g)),
                  pl.BlockSpec((1, n_half, 128), lambda b, g, i: (b, 0, g)),
                  pl.BlockSpec((1, n_half, 128), lambda b, g, i: (b, 0, 4 + g)),
                  pl.BlockSpec((1, n_half, 128), lambda b, g, i: (b, 0, 8 + g)),
                  pl.BlockSpec((1, n_half, 128), lambda b, g, i: (b, 0, 12 + g)),
                  pl.BlockSpec((8, 128), lambda b, g, i: (0, 0)),
                  pl.BlockSpec((128, 128), lambda b, g, i: (0, 0)),
                  pl.BlockSpec((t, 128), lambda b, g, i: (0, 0))],
        out_specs=pl.BlockSpec((tq, gw), lambda b, g, i: (b * nq + i, g)),
        out_shape=jax.ShapeDtypeStruct((nb * t, D_A), BF16),
        scratch_shapes=[pltpu.VMEM((t, 2 * HEAD_DIM), BF16),
                        pltpu.VMEM((n_half, HEAD_DIM), BF16),
                        pltpu.VMEM((n_half, HEAD_DIM), BF16)],
        compiler_params=_cp(("parallel", "parallel", "arbitrary")),
        name="nsa_prompt",
    )(pb, pb, kvb_rows, kvb_rows, kvb_win, kvb_win, ab, ab, ab, ab, pos8, ovt, emat)


def _dec1_kernel(q_ref, ab_ref, pos_ref, ov_ref, cwin_ref, knew_ref, vnew_ref,
                 ocmp_ref, owin_ref, idx_ref, *, n_slc):
    n_half = ab_ref.shape[1]
    nlane = ov_ref.shape[1]
    lane = lax.broadcasted_iota(jnp.int32, (8, nlane), 1)
    row8 = lax.broadcasted_iota(jnp.int32, (8, nlane), 0)
    imp_all = jnp.zeros((8, nlane), F32)
    for g in range(N_KV):
        sl = slice(g * HEAD_DIM, (g + 1) * HEAD_DIM)
        qg = (q_ref[0, g] * SCALE).astype(BF16)
        cmpk = (ab_ref[0, :, sl] + pltpu.roll(ab_ref[0, :, 512 + g * 128:512 + (g + 1) * 128], n_half - 1, 0)
                + pos_ref[0:1, :]).astype(BF16)
        cmpv = (ab_ref[0, :, 1024 + g * 128:1024 + (g + 1) * 128]
                + pltpu.roll(ab_ref[0, :, 1536 + g * 128:1536 + (g + 1) * 128], n_half - 1, 0)
                + pos_ref[1:2, :]).astype(BF16)
        n_l = lax.broadcasted_iota(jnp.int32, (1, n_half), 1)
        cm = n_l < n_half - 1
        s = jnp.where(cm, _dot_nt(qg, cmpk), NEG)
        e = jnp.where(cm, jnp.exp(s - jnp.max(s, axis=-1, keepdims=True)), 0.0)
        p = e / jnp.sum(e, axis=-1, keepdims=True)
        ocmp_ref[0, g] = _dot(p.astype(BF16), cmpv)
        ps = p[0:1] + p[1:2] + p[2:3]
        ps8 = jnp.broadcast_to(ps, (8, n_half))
        ph = ps8.astype(BF16)
        plo = (ps8 - ph.astype(F32)).astype(BF16)
        imp = _dot(ph, ov_ref[...]) + _dot(plo, ov_ref[...])
        imp_all = jnp.where(row8 == g, imp, imp_all)

        kc = cwin_ref[0, :, sl].astype(BF16)
        vc = cwin_ref[0, :, 512 + g * 128:512 + (g + 1) * 128].astype(BF16)
        kn = knew_ref[0, g][0:1, :].astype(BF16).astype(F32)
        vn = vnew_ref[0, g][0:1, :].astype(BF16).astype(F32)
        w_l = lax.broadcasted_iota(jnp.int32, (1, kc.shape[0]), 1)
        sc = jnp.where(w_l >= 1, _dot_nt(qg, kc), NEG)
        sn = jnp.sum(qg.astype(F32) * kn, axis=-1, keepdims=True)
        m = jnp.maximum(jnp.max(sc, axis=-1, keepdims=True), sn)
        ec = jnp.exp(sc - m)
        en = jnp.exp(sn - m)
        lw = jnp.sum(ec, axis=-1, keepdims=True) + en
        owin_ref[0, g] = _dot((ec / lw).astype(BF16), vc) + (en / lw).astype(BF16).astype(F32) * vn

    cur = n_slc - 1
    forced = (lane == 0) | (lane == cur) | (lane == cur - 1)
    score = jnp.where(forced, FORCE, jnp.where(lane < n_slc, imp_all, -3.0 * FORCE))
    idx = jnp.zeros((8, 128), F32)
    lane128 = lax.broadcasted_iota(jnp.int32, (8, 128), 1)
    lane_f = lane.astype(F32)
    for it in range(TOPK):
        m = jnp.max(score, axis=-1, keepdims=True)
        ix = jnp.min(jnp.where(score == m, lane_f, 1e6), axis=-1, keepdims=True)
        idx = jnp.where(lane128 == it, ix, idx)
        score = jnp.where(lane_f == ix, -4.0 * FORCE, score)
    idx_ref[0] = idx.astype(jnp.int32)


def _dec1(q4, ab, pos8, ov, cwin, knew4, vnew4, *, n_slc):
    nb = q4.shape[0]
    n_half = ab.shape[1]
    hg = pl.BlockSpec((1, N_KV, 8, HEAD_DIM), lambda b: (b, 0, 0, 0))
    return pl.pallas_call(
        functools.partial(_dec1_kernel, n_slc=n_slc),
        grid=(nb,),
        in_specs=[hg,
                  pl.BlockSpec((1, n_half, 2048), lambda b: (b, 0, 0)),
                  pl.BlockSpec((8, 128), lambda b: (0, 0)),
                  pl.BlockSpec(ov.shape, lambda b: (0, 0)),
                  pl.BlockSpec((1, WINDOW, 1024), lambda b: (b, 0, 0)),
                  hg, hg],
        out_specs=[hg, hg, pl.BlockSpec((1, 8, 128), lambda b: (b, 0, 0))],
        out_shape=[jax.ShapeDtypeStruct((nb, N_KV, 8, HEAD_DIM), F32),
                   jax.ShapeDtypeStruct((nb, N_KV, 8, HEAD_DIM), F32),
                   jax.ShapeDtypeStruct((nb, 8, 128), jnp.int32)],
        compiler_params=_cp(("parallel",)),
        name="dec_cmp_win",
    )(q4, ab, pos8, ov, cwin, knew4, vnew4)


def _dec2_kernel(pt_ref, si_ref, *refs, new_blk):
    del pt_ref
    kv_refs = refs[:TOPK]
    q_ref, knew_ref, vnew_ref, ocmp_ref, owin_ref, gate_ref, y_ref = refs[TOPK:]
    b = pl.program_id(0)
    g = pl.program_id(1)
    qg = (q_ref[0, 0] * SCALE).astype(BF16)
    row = lax.broadcasted_iota(jnp.int32, (SEL_BLOCK, HEAD_DIM), 0)
    lane = lax.broadcasted_iota(jnp.int32, (1, TOPK * SEL_BLOCK), 1)
    kn = knew_ref[0, 0][0:1, :]
    vn = vnew_ref[0, 0][0:1, :]
    ks, vs = [], []
    masked = jnp.zeros((1, TOPK * SEL_BLOCK), jnp.bool_)
    lane_blk = lax.shift_right_logical(lane, 6)
    lane_off = lane & (SEL_BLOCK - 1)
    for kk in range(TOPK):
        is_new = si_ref[(b * N_KV + g) * TOPK + kk] == new_blk
        sub = row == jnp.where(is_new, 0, -1)
        k_blk = kv_refs[kk][pl.ds(2 * N_KV + g, SEL_BLOCK, stride=16), :]
        v_blk = kv_refs[kk][pl.ds(3 * N_KV + g, SEL_BLOCK, stride=16), :]
        ks.append(jnp.where(sub, kn, k_blk).astype(BF16))
        vs.append(jnp.where(sub, vn, v_blk).astype(BF16))
        masked = masked | ((lane_blk == jnp.where(is_new, kk, -1)) & (lane_off > 0))
    kall = jnp.concatenate(ks, axis=0)
    vall = jnp.concatenate(vs, axis=0)
    s = jnp.where(masked, NEG, _dot_nt(qg, kall))
    e = jnp.exp(s - jnp.max(s, axis=-1, keepdims=True))
    p = e / jnp.sum(e, axis=-1, keepdims=True)
    o_sel = _dot(p.astype(BF16), vall)
    gs = _sigmoid(gate_ref[0, 0])
    y_ref[0, 0] = gs[:, 0:1] * ocmp_ref[0, 0] + gs[:, 1:2] * o_sel + gs[:, 2:3] * owin_ref[0, 0]


def _dec2(pt, sidx, cache2, q4, knew4, vnew4, ocmp, owin, gate4, *, new_blk):
    nb = q4.shape[0]
    last = new_blk - 1

    def half_page(b, g, pt, si, kk):
        j = jnp.minimum(si[(b * N_KV + g) * TOPK + kk], last)
        return pt[b, lax.shift_right_logical(j, 1)] * 2 + (j & 1)

    kv_specs = [pl.BlockSpec((SEL_BLOCK * 16, 128), lambda b, g, pt, si, kk=kk: (half_page(b, g, pt, si, kk), 0))
                for kk in range(TOPK)]
    hg = pl.BlockSpec((1, 1, 8, HEAD_DIM), lambda b, g, pt, si: (b, g, 0, 0))
    return pl.pallas_call(
        functools.partial(_dec2_kernel, new_blk=new_blk),
        grid_spec=pltpu.PrefetchScalarGridSpec(
            num_scalar_prefetch=2,
            grid=(nb, N_KV),
            in_specs=kv_specs + [hg] * 6,
            out_specs=hg),
        out_shape=jax.ShapeDtypeStruct((nb, N_KV, 8, HEAD_DIM), F32),
        compiler_params=_cp(("parallel", "parallel")),
        name="dec_sel",
    )(pt, sidx, *([cache2] * TOPK), q4, knew4, vnew4, ocmp, owin, gate4)


def _pad_rows(a, rows):
    return jnp.pad(a, ((0, rows - a.shape[0]),) + ((0, 0),) * (a.ndim - 1))


def _to_hg(a2d, nb):
    g = a2d.shape[1] // (N_KV * HEAD_DIM)
    a = a2d.reshape(nb, N_KV, g, HEAD_DIM)
    return jnp.pad(a, ((0, 0), (0, 0), (0, 8 - g), (0, 0)))


def kernel(x_prompt, x_sample, cache_kv, cache_win, cache_mem, state_conv_mix, state_conv_ffn, page_table,
           mem_prompt, ln_g, ln_b, w_in_a, conv_a_w, w_in_b, w_o, w_mem_kv, w_up, ffn_conv_w, w_down,
           w_kv_shared, cmp_pos, w_cmp):
    nb, t = x_prompt.shape[:2]
    db = x_sample.shape[0]
    m_p = nb * t
    ms = 16
    tm = 512
    tpb = t // tm
    past = page_table.shape[1] * cache_kv.shape[1]

    w_in_a_b = w_in_a.astype(BF16)
    hq = N_HEADS * HEAD_DIM
    hgate = hq + 3 * N_HEADS
    wgate = w_in_b[:, :, hq:hgate].reshape(2, D_MODEL, 3, N_KV, GROUP)
    wgate = jnp.transpose(wgate, (0, 1, 3, 2, 4)).reshape(2, D_MODEL, N_KV, 3 * GROUP)
    wgate = jnp.pad(wgate, ((0, 0), (0, 0), (0, 0), (0, 128 - 3 * GROUP))).reshape(2, D_MODEL, N_KV * 128)
    w_qg_b = jnp.concatenate([w_in_b[:, :, :hq], wgate], axis=-1).astype(BF16)
    w_qm_b = jnp.concatenate([w_in_a[:, :, 3 * D_A:], w_in_b[:, :, hgate:]], axis=0).astype(BF16)
    w_o_b = w_o.astype(BF16)
    padf = D_FF_PAD - D_FF
    w_up_b = jnp.concatenate([jnp.pad(w_up[:, :, :D_FF], ((0, 0), (0, 0), (0, padf))),
                              jnp.pad(w_up[:, :, D_FF:], ((0, 0), (0, 0), (0, padf)))], axis=-1).astype(BF16)
    w_down_b = jnp.pad(w_down, ((0, 0), (0, padf), (0, 0))).astype(BF16)
    ffn_cw = jnp.pad(ffn_conv_w, ((0, 0), (0, 0), (0, padf)))
    w_kv_b = w_kv_shared.astype(BF16)
    w_mem_b = w_mem_kv.astype(BF16)
    w_cmp2 = jnp.concatenate([w_cmp[:, :CMP_STRIDE].reshape(2, 2048, 128),
                              w_cmp[:, CMP_STRIDE:].reshape(2, 2048, 128)], axis=-1).astype(BF16)
    pos8 = _posterm(jnp.broadcast_to(cmp_pos.reshape(2, 1, CMP_LEN * HEAD_DIM), (2, 8, CMP_LEN * HEAD_DIM)),
                    w_cmp.reshape(2, CMP_LEN * HEAD_DIM, HEAD_DIM))
    g2 = ln_g.reshape(DEPTH, 2, 1, D_MODEL)
    b2 = ln_b.reshape(DEPTH, 2, 1, D_MODEL)

    def overlap(n_cmp, n_lanes_j, n_rows):
        n = jnp.arange(n_rows)[:, None]
        j = jnp.arange(n_lanes_j)[None, :]
        return ((n >= 4 * j - 1) & (n <= 4 * j + 3) & (n < n_cmp)).astype(BF16)

    ovt_p = overlap(t // CMP_STRIDE - 1, 128, 128).T
    emat = (jnp.arange(t)[:, None] // SEL_BLOCK == jnp.arange(128)[None, :]).astype(BF16)
    n_half_s = past // CMP_STRIDE
    n_slc_s = past // SEL_BLOCK + 1
    ov_s = overlap(n_half_s - 1, 384, n_half_s)

    x = x_prompt.reshape(m_p, D_MODEL)
    mem_x = mem_prompt.reshape(nb * N_MEM, D_MODEL)
    mem_kv = [_mm(mem_x, w_mem_b[l], tm=512, tn=512, name="mm_memkv").reshape(nb, N_MEM, 2 * MEM_W) for l in range(DEPTH)]
    zeros_mix = jnp.zeros((nb, 8, D_A), F32)
    zeros_ffn = jnp.zeros((nb, 8, D_FF_PAD), F32)
    conv_p, ffn_p = [], []

    def tail(x, l, y, memkv_l, seq_prev8):
        mo = _memattn(x.reshape(nb, t, D_MODEL), w_qm_b[l], memkv_l, tt=tm, out_dtype=BF16).reshape(m_p, MEM_W)
        x = _oproj(x, y, mo, w_o_b[l], g2[l, 0], b2[l, 0], tm=tm)
        x, st = _ffn_seq(x, w_up_b[l], ffn_cw[l], w_down_b[l], seq_prev8, g2[l, 1], b2[l, 1],
                         tm=tm, tf=512, tpb=tpb)
        return x, st

    for l in range(N_A):
        y, st = _amix_seq(x, w_in_a_b[l], conv_a_w[l], zeros_mix, tm=tm, tn=512, tpb=tpb)
        conv_p.append(st[tpb - 1::tpb, 6:8])
        x, fs = tail(x, l, y, mem_kv[l], zeros_ffn)
        ffn_p.append(fs[tpb - 1::tpb, 6:8, :D_FF])

    kv_tr, kvb_rows = _kvrows(x, w_kv_b[:, :2048], tm=tm, tn=512)
    kv_win, kvb_win = _mm(x, w_kv_b[:, 2048:], tm=tm, tn=512, out_dtypes=(F32, BF16), name="mm_kvwin")
    pt_p = jnp.arange(m_p // PAGE, dtype=jnp.int32).reshape(nb, t // PAGE)
    ab_p = _compress(kv_tr.reshape(m_p, 16, HEAD_DIM), pt_p, w_cmp2, npg=t // PAGE)

    for j in range(DEPTH - N_A):
        l = N_A + j
        pb = _mm(x, w_qg_b[j], tm=tm, tn=512, name="mm_qgate")
        y = _nsa_prompt(pb, kvb_rows, kvb_win, ab_p, pos8, ovt_p, emat, nb=nb, t=t, tq=128, ck=512)
        x, fs = tail(x, l, y, mem_kv[l], zeros_ffn)
        ffn_p.append(fs[tpb - 1::tpb, 6:8, :D_FF])
    y_prompt = x.reshape(nb, t, D_MODEL)

    xs = _pad_rows(x_sample.reshape(db, D_MODEL), ms)
    cmem = cache_mem.reshape(DEPTH, db, N_MEM, 2 * MEM_W)
    conv_s, ffn_s = [], []

    def tail_s(xs, l, y):
        xrep = jnp.broadcast_to(xs[:db, None, :], (db, 8, D_MODEL))
        mo = _pad_rows(_memattn(xrep, w_qm_b[l], cmem[l], tt=8, out_dtype=F32)[:, 0], ms)
        xs = _oproj(xs, y, mo, w_o_b[l], g2[l, 0], b2[l, 0], tm=ms)
        p0 = _pad_rows(jnp.pad(state_conv_ffn[l, :, 0], ((0, 0), (0, padf))), ms)
        p1 = _pad_rows(jnp.pad(state_conv_ffn[l, :, 1], ((0, 0), (0, padf))), ms)
        xs, uz = _ffn_tok(xs, w_up_b[l], ffn_cw[l], w_down_b[l], p0, p1, g2[l, 1], b2[l, 1], tf=512)
        ffn_s.append(jnp.stack([state_conv_ffn[l, :, 1], uz[:db, :D_FF]], axis=1))
        return xs

    for l in range(N_A):
        p0 = _pad_rows(state_conv_mix[l, :, 0], ms)
        p1 = _pad_rows(state_conv_mix[l, :, 1], ms)
        y, cu = _amix_tok(xs, w_in_a_b[l], conv_a_w[l], p0, p1, tn=512)
        conv_s.append(jnp.stack([state_conv_mix[l, :, 1], cu[:db]], axis=1))
        xs = tail_s(xs, l, y)

    kv_s = _mm(xs, w_kv_b, tm=ms, tn=512, name="mm_kv_tok")[:db]
    n_tok = cache_kv.shape[0] * PAGE
    ab_s = _compress(cache_kv.reshape(n_tok, 16, HEAD_DIM), page_table, w_cmp2, npg=16)
    cache2 = cache_kv.reshape(n_tok * 16, HEAD_DIM)
    cwin = cache_win.reshape(db, WINDOW, 1024)
    ksel_new = _to_hg(kv_s[:, 1024:1536], db)
    vsel_new = _to_hg(kv_s[:, 1536:2048], db)
    kwin_new = _to_hg(kv_s[:, 2048:2560], db)
    vwin_new = _to_hg(kv_s[:, 2560:3072], db)

    for j in range(DEPTH - N_A):
        l = N_A + j
        pbs = _mm(xs, w_qg_b[j], tm=ms, tn=512, name="mm_qgate_tok")[:db]
        q4 = _to_hg(pbs[:, :hq], db)
        gate4 = pbs[:, hq:].reshape(db, N_KV, 128)[:, :, :3 * GROUP].reshape(db, N_KV, 3, GROUP)
        gate4 = jnp.pad(gate4.transpose(0, 1, 3, 2), ((0, 0), (0, 0), (0, 8 - GROUP), (0, 128 - 3)))
        ocmp, owin, idx = _dec1(q4, ab_s, pos8, ov_s, cwin, kwin_new, vwin_new, n_slc=n_slc_s)
        sidx = idx[:, :N_KV, :TOPK].reshape(db * N_KV * TOPK)
        y4 = _dec2(page_table, sidx, cache2, q4, ksel_new, vsel_new, ocmp, owin, gate4, new_blk=n_slc_s - 1)
        y = _pad_rows(y4[:, :, :GROUP].reshape(db, D_A), ms).astype(BF16)
        xs = tail_s(xs, l, y)
    y_sample = xs[:db].reshape(db, 1, D_MODEL)

    wb = cache_win.shape[1]
    win_new = kv_s[:, 2048:].reshape(db, 1, 2, N_KV, HEAD_DIM)
    return (y_prompt, y_sample,
            kv_tr.reshape(nb, t, 4, N_KV, HEAD_DIM),
            kv_win.reshape(nb, t, 2, N_KV, HEAD_DIM)[:, max(t - WINDOW, 0):],
            jnp.stack(mem_kv).reshape(DEPTH, nb, N_MEM, 2, 4, HEAD_DIM),
            jnp.stack(conv_p),
            jnp.stack(ffn_p),
            kv_s[:, :2048].reshape(db, 1, 4, N_KV, HEAD_DIM),
            jnp.concatenate([cache_win, win_new], axis=1)[:, -wb:],
            jnp.stack(conv_s),
            jnp.stack(ffn_s))
```

```python
import functools

import jax
import jax.numpy as jnp
from jax import lax
from jax.experimental import pallas as pl
from jax.experimental.pallas import tpu as pltpu

F32 = jnp.float32
BF16 = jnp.bfloat16

D_MODEL = 2048
DEPTH = 4
HEAD_DIM = 128
N_A = 2
D_A = 1536
N_HEADS = 12
N_KV = 4
GROUP = 3
MEM_W = 512
N_MEM = 256
D_FF = 5504
D_FF_PAD = 5632
CMP_STRIDE = 16
CMP_LEN = 32
SEL_BLOCK = 64
TOPK = 16
WINDOW = 512
PAGE = 128
ALPHA = (2 * DEPTH) ** 0.25
LN_EPS = 1e-5
NEG = -1e30
FORCE = 1e9
SCALE = HEAD_DIM ** -0.5

VMEM_LIMIT = 56 * 1024 * 1024


def _cp(sem):
    return pltpu.CompilerParams(dimension_semantics=sem, vmem_limit_bytes=VMEM_LIMIT)


def _dot(a, b):
    return jnp.dot(a, b, preferred_element_type=F32)


def _wspec(block, index_map, layer=None):
    if layer is None:
        return pl.BlockSpec(block, index_map)
    return pl.BlockSpec((None,) + block, lambda *a: (layer,) + index_map(*a))


def _dot_nt(a, b):
    return lax.dot_general(a, b, (((1,), (1,)), ((), ())), preferred_element_type=F32)


def _layer_norm(v, g, b):
    mu = jnp.mean(v, axis=-1, keepdims=True)
    d = v - mu
    var = jnp.mean(d * d, axis=-1, keepdims=True)
    return d * lax.rsqrt(var + LN_EPS) * g + b


def _sigmoid(x):
    return 1.0 / (1.0 + jnp.exp(-x))


def _fold(x, op):
    r = x[:, 0:128]
    for i in range(1, x.shape[1] // 128):
        r = op(r, x[:, i * 128:(i + 1) * 128])
    return r


def _mm_kernel(x_ref, w_ref, *rest, n_out):
    o_refs = rest[:n_out]
    xb_ref = rest[n_out]

    @pl.when(pl.program_id(1) == 0)
    def _():
        xb_ref[...] = x_ref[...].astype(BF16)

    r = _dot(xb_ref[...], w_ref[...])
    for o in o_refs:
        o[...] = r.astype(o.dtype)


def _mm(x, w, *, tm, tn, n=None, col0=0, layer=None, out_dtypes=(F32,), name="mm"):
    m, k = x.shape
    n = w.shape[-1] if n is None else n
    outs = pl.pallas_call(
        functools.partial(_mm_kernel, n_out=len(out_dtypes)),
        grid=(m // tm, n // tn),
        in_specs=[pl.BlockSpec((tm, k), lambda i, j: (i, 0)),
                  _wspec((k, tn), lambda i, j: (0, col0 + j), layer)],
        out_specs=[pl.BlockSpec((tm, tn), lambda i, j: (i, j)) for _ in out_dtypes],
        out_shape=[jax.ShapeDtypeStruct((m, n), d) for d in out_dtypes],
        scratch_shapes=[pltpu.VMEM((tm, k), BF16)],
        compiler_params=_cp(("parallel", "arbitrary")),
        name=name,
    )(x, w)
    return outs if len(out_dtypes) > 1 else outs[0]


def _kvrows_kernel(x_ref, w_ref, tr_ref, ob_ref, xb_ref):
    j = pl.program_id(1)

    @pl.when(j == 0)
    def _():
        xb_ref[...] = x_ref[...].astype(BF16)

    r = _dot(xb_ref[...], w_ref[...])
    ob_ref[...] = r.astype(BF16)
    tm = r.shape[0]
    for c in range(r.shape[1] // HEAD_DIM):
        tr_ref[pl.ds(j * (r.shape[1] // HEAD_DIM) + c, tm, stride=16), :] = r[:, c * HEAD_DIM:(c + 1) * HEAD_DIM]


def _kvrows(x, w, *, tm, tn):
    m, k = x.shape
    n = 16 * HEAD_DIM
    return pl.pallas_call(
        _kvrows_kernel,
        grid=(m // tm, n // tn),
        in_specs=[pl.BlockSpec((tm, k), lambda i, j: (i, 0)),
                  pl.BlockSpec((k, tn), lambda i, j: (0, j))],
        out_specs=[pl.BlockSpec((tm * 16, HEAD_DIM), lambda i, j: (i, 0)),
                   pl.BlockSpec((tm, tn), lambda i, j: (i, j))],
        out_shape=[jax.ShapeDtypeStruct((m * 16, HEAD_DIM), F32),
                   jax.ShapeDtypeStruct((m, n), BF16)],
        scratch_shapes=[pltpu.VMEM((tm, k), BF16)],
        compiler_params=_cp(("parallel", "arbitrary")),
        name="kvrows",
    )(x, w)


def _conv_rows(v, cw_ref, tail8):
    row = lax.broadcasted_iota(jnp.int32, v.shape, 0)
    t0 = tail8[6:7, :]
    t1 = tail8[7:8, :]
    s1 = jnp.where(row == 0, t1, pltpu.roll(v, 1, 0))
    s2 = jnp.where(row == 0, t0, jnp.where(row == 1, t1, pltpu.roll(v, 2, 0)))
    return cw_ref[0:1, :] * s2 + cw_ref[1:2, :] * s1 + cw_ref[2:3, :] * v


def _amix_seq_kernel(x_ref, wu_ref, wb_ref, wc_ref, cw_ref, prev_ref, y_ref, st_ref,
                     xb_ref, carry_ref, *, tpb):
    i = pl.program_id(0)
    j = pl.program_id(1)

    @pl.when(j == 0)
    def _():
        xb_ref[...] = x_ref[...].astype(BF16)

    xb = xb_ref[...]
    cu = _dot(xb, wc_ref[...]) * _dot(xb, wu_ref[...])
    tail8 = jnp.where((i % tpb) == 0, prev_ref[0], carry_ref[j])
    conv = _conv_rows(cu, cw_ref, tail8)
    y_ref[...] = (_dot(xb, wb_ref[...]) * conv).astype(y_ref.dtype)
    last8 = cu[cu.shape[0] - 8:, :]
    carry_ref[j] = last8
    st_ref[0] = last8


def _amix_seq(x, w_in, cw, prev8, *, layer, tm, tn, tpb):
    m = x.shape[0]
    nj = D_A // tn
    return pl.pallas_call(
        functools.partial(_amix_seq_kernel, tpb=tpb),
        grid=(m // tm, nj),
        in_specs=[pl.BlockSpec((tm, D_MODEL), lambda i, j: (i, 0)),
                  _wspec((D_MODEL, tn), lambda i, j: (0, j), layer),
                  _wspec((D_MODEL, tn), lambda i, j: (0, nj + j), layer),
                  _wspec((D_MODEL, tn), lambda i, j: (0, 2 * nj + j), layer),
                  _wspec((3, tn), lambda i, j: (0, j), layer),
                  pl.BlockSpec((1, 8, tn), lambda i, j: (i // tpb, 0, j))],
        out_specs=[pl.BlockSpec((tm, tn), lambda i, j: (i, j)),
                   pl.BlockSpec((1, 8, tn), lambda i, j: (i, 0, j))],
        out_shape=[jax.ShapeDtypeStruct((m, D_A), BF16),
                   jax.ShapeDtypeStruct((m // tm, 8, D_A), F32)],
        scratch_shapes=[pltpu.VMEM((tm, D_MODEL), BF16),
                        pltpu.VMEM((nj, 8, tn), F32)],
        compiler_params=_cp(("arbitrary", "arbitrary")),
        name="amix_seq",
    )(x, w_in, w_in, w_in, cw, prev8)


def _amix_tok_kernel(x_ref, wu_ref, wb_ref, wc_ref, cw_ref, p0_ref, p1_ref, y_ref, cu_ref):
    xb = x_ref[...].astype(BF16)
    cu = _dot(xb, wc_ref[...]) * _dot(xb, wu_ref[...])
    conv = cw_ref[0:1, :] * p0_ref[...] + cw_ref[1:2, :] * p1_ref[...] + cw_ref[2:3, :] * cu
    y_ref[...] = (_dot(xb, wb_ref[...]) * conv).astype(y_ref.dtype)
    cu_ref[...] = cu


def _amix_tok(x, w_in, cw, p0, p1, *, layer, tn):
    m = x.shape[0]
    nj = D_A // tn
    return pl.pallas_call(
        _amix_tok_kernel,
        grid=(nj,),
        in_specs=[pl.BlockSpec((m, D_MODEL), lambda j: (0, 0)),
                  _wspec((D_MODEL, tn), lambda j: (0, j), layer),
                  _wspec((D_MODEL, tn), lambda j: (0, nj + j), layer),
                  _wspec((D_MODEL, tn), lambda j: (0, 2 * nj + j), layer),
                  _wspec((3, tn), lambda j: (0, j), layer),
                  pl.BlockSpec((m, tn), lambda j: (0, j)),
                  pl.BlockSpec((m, tn), lambda j: (0, j))],
        out_specs=[pl.BlockSpec((m, tn), lambda j: (0, j)),
                   pl.BlockSpec((m, tn), lambda j: (0, j))],
        out_shape=[jax.ShapeDtypeStruct((m, D_A), BF16),
                   jax.ShapeDtypeStruct((m, D_A), F32)],
        compiler_params=_cp(("parallel",)),
        name="amix_tok",
    )(x, w_in, w_in, w_in, cw, p0, p1)


def _memattn_kernel(x_ref, wq_ref, mk_ref, mv_ref, o_ref):
    xb = x_ref[0].astype(BF16)
    qm = _dot(xb, wq_ref[...]) * SCALE
    for h in range(4):
        sl = slice(h * HEAD_DIM, (h + 1) * HEAD_DIM)
        q = qm[:, sl].astype(BF16)
        s = _dot_nt(q, mk_ref[0, :, sl].astype(BF16))
        e = jnp.exp(s - jnp.max(s, axis=-1, keepdims=True))
        p = e / jnp.sum(e, axis=-1, keepdims=True)
        o_ref[0, :, sl] = _dot(p.astype(BF16), mv_ref[0, :, sl].astype(BF16)).astype(o_ref.dtype)


def _memattn(x3, wq, memkv, *, layer, tt, out_dtype):
    nb, t, _ = x3.shape
    return pl.pallas_call(
        _memattn_kernel,
        grid=(nb, t // tt),
        in_specs=[pl.BlockSpec((1, tt, D_MODEL), lambda b, i: (b, i, 0)),
                  _wspec((D_MODEL, MEM_W), lambda b, i: (0, 0), layer),
                  pl.BlockSpec((1, N_MEM, MEM_W), lambda b, i: (b, 0, 0)),
                  pl.BlockSpec((1, N_MEM, MEM_W), lambda b, i: (b, 0, 1))],
        out_specs=pl.BlockSpec((1, tt, MEM_W), lambda b, i: (b, i, 0)),
        out_shape=jax.ShapeDtypeStruct((nb, t, MEM_W), out_dtype),
        compiler_params=_cp(("parallel", "parallel")),
        name="memattn",
    )(x3, wq, memkv, memkv)


def _oproj_kernel(x_ref, y_ref, m_ref, wy_ref, wm_ref, g_ref, b_ref, o_ref):
    acc = _dot(y_ref[...].astype(BF16), wy_ref[...]) + _dot(m_ref[...].astype(BF16), wm_ref[...])
    o_ref[...] = _layer_norm(ALPHA * x_ref[...] + acc, g_ref[...], b_ref[...])


def _oproj(x, y, mo, wo, g, b, *, layer, tm):
    m = x.shape[0]
    return pl.pallas_call(
        _oproj_kernel,
        grid=(m // tm,),
        in_specs=[pl.BlockSpec((tm, D_MODEL), lambda i: (i, 0)),
                  pl.BlockSpec((tm, D_A), lambda i: (i, 0)),
                  pl.BlockSpec((tm, MEM_W), lambda i: (i, 0)),
                  _wspec((D_A, D_MODEL), lambda i: (0, 0), layer),
                  _wspec((MEM_W, D_MODEL), lambda i: (D_A // MEM_W, 0), layer),
                  pl.BlockSpec((1, D_MODEL), lambda i: (0, 0)),
                  pl.BlockSpec((1, D_MODEL), lambda i: (0, 0))],
        out_specs=pl.BlockSpec((tm, D_MODEL), lambda i: (i, 0)),
        out_shape=jax.ShapeDtypeStruct((m, D_MODEL), F32),
        compiler_params=_cp(("parallel",)),
        name="oproj_ln",
    )(x, y, mo, wo, wo, g, b)


def _ffn_seq_kernel(x_ref, wz_ref, wg_ref, cw_ref, wd_ref, prev_ref, g_ref, b_ref,
                    o_ref, st_ref, xb_ref, acc_ref, carry_ref, *, tpb):
    i = pl.program_id(0)
    j = pl.program_id(1)

    @pl.when(j == 0)
    def _():
        xb_ref[...] = x_ref[...].astype(BF16)
        acc_ref[...] = jnp.zeros_like(acc_ref)

    xb = xb_ref[...]
    uz = _dot(xb, wz_ref[...])
    tail8 = jnp.where((i % tpb) == 0, prev_ref[0], carry_ref[j])
    z = _conv_rows(uz, cw_ref, tail8)
    h = z * _sigmoid(z) * _dot(xb, wg_ref[...])
    acc_ref[...] += _dot(h.astype(BF16), wd_ref[...])
    last8 = uz[uz.shape[0] - 8:, :]
    carry_ref[j] = last8
    st_ref[0] = last8

    @pl.when(j == pl.num_programs(1) - 1)
    def _():
        o_ref[...] = _layer_norm(ALPHA * x_ref[...] + acc_ref[...], g_ref[...], b_ref[...])


def _ffn_seq(x, wz, wg, cw, wd, prev8, g, b, *, layer, tm, tf, tpb):
    m = x.shape[0]
    nj = D_FF_PAD // tf
    return pl.pallas_call(
        functools.partial(_ffn_seq_kernel, tpb=tpb),
        grid=(m // tm, nj),
        in_specs=[pl.BlockSpec((tm, D_MODEL), lambda i, j: (i, 0)),
                  _wspec((D_MODEL, tf), lambda i, j: (0, j), layer),
                  _wspec((D_MODEL, tf), lambda i, j: (0, j), layer),
                  _wspec((3, tf), lambda i, j: (0, j), layer),
                  _wspec((tf, D_MODEL), lambda i, j: (j, 0), layer),
                  pl.BlockSpec((1, 8, tf), lambda i, j: (i // tpb, 0, j)),
                  pl.BlockSpec((1, D_MODEL), lambda i, j: (0, 0)),
                  pl.BlockSpec((1, D_MODEL), lambda i, j: (0, 0))],
        out_specs=[pl.BlockSpec((tm, D_MODEL), lambda i, j: (i, 0)),
                   pl.BlockSpec((1, 8, tf), lambda i, j: (i, 0, j))],
        out_shape=[jax.ShapeDtypeStruct((m, D_MODEL), F32),
                   jax.ShapeDtypeStruct((m // tm, 8, D_FF_PAD), F32)],
        scratch_shapes=[pltpu.VMEM((tm, D_MODEL), BF16),
                        pltpu.VMEM((tm, D_MODEL), F32),
                        pltpu.VMEM((nj, 8, tf), F32)],
        compiler_params=_cp(("arbitrary", "arbitrary")),
        name="ffn_seq",
    )(x, wz, wg, cw, wd, prev8, g, b)


def _ffn_tok_kernel(x_ref, wz_ref, wg_ref, cw_ref, wd_ref, p0_ref, p1_ref, g_ref, b_ref,
                    o_ref, uz_ref, acc_ref):
    j = pl.program_id(0)

    @pl.when(j == 0)
    def _():
        acc_ref[...] = jnp.zeros_like(acc_ref)

    xb = x_ref[...].astype(BF16)
    uz = _dot(xb, wz_ref[...])
    z = cw_ref[0:1, :] * p0_ref[...] + cw_ref[1:2, :] * p1_ref[...] + cw_ref[2:3, :] * uz
    h = z * _sigmoid(z) * _dot(xb, wg_ref[...])
    acc_ref[...] += _dot(h.astype(BF16), wd_ref[...])
    uz_ref[...] = uz

    @pl.when(j == pl.num_programs(0) - 1)
    def _():
        o_ref[...] = _layer_norm(ALPHA * x_ref[...] + acc_ref[...], g_ref[...], b_ref[...])


def _ffn_tok(x, wz, wg, cw, wd, p0, p1, g, b, *, layer, tf):
    m = x.shape[0]
    nj = D_FF_PAD // tf
    return pl.pallas_call(
        _ffn_tok_kernel,
        grid=(nj,),
        in_specs=[pl.BlockSpec((m, D_MODEL), lambda j: (0, 0)),
                  _wspec((D_MODEL, tf), lambda j: (0, j), layer),
                  _wspec((D_MODEL, tf), lambda j: (0, j), layer),
                  _wspec((3, tf), lambda j: (0, j), layer),
                  _wspec((tf, D_MODEL), lambda j: (j, 0), layer),
                  pl.BlockSpec((m, tf), lambda j: (0, j)),
                  pl.BlockSpec((m, tf), lambda j: (0, j)),
                  pl.BlockSpec((1, D_MODEL), lambda j: (0, 0)),
                  pl.BlockSpec((1, D_MODEL), lambda j: (0, 0))],
        out_specs=[pl.BlockSpec((m, D_MODEL), lambda j: (0, 0)),
                   pl.BlockSpec((m, tf), lambda j: (0, j))],
        out_shape=[jax.ShapeDtypeStruct((m, D_MODEL), F32),
                   jax.ShapeDtypeStruct((m, D_FF_PAD), F32)],
        scratch_shapes=[pltpu.VMEM((m, D_MODEL), F32)],
        compiler_params=_cp(("arbitrary",)),
        name="ffn_tok",
    )(x, wz, wg, cw, wd, p0, p1, g, b)


def _compress_kernel(pt_ref, *refs, npg):
    del pt_ref
    page_refs = refs[:npg]
    w_ref, o_ref, x2_ref = refs[npg:]
    nh = npg * 8
    for pg in range(npg):
        for r in range(PAGE):
            n = pg * 8 + r // CMP_STRIDE
            x2_ref[r % CMP_STRIDE, n * 8:n * 8 + 8, :] = page_refs[pg][r]
    for c in range(2):
        xs = [jnp.concatenate([x2_ref[s, pl.ds(c * N_KV + g, nh, stride=8), :] for s in range(CMP_STRIDE)], axis=1)
              for g in range(N_KV)]
        r = _dot(jnp.concatenate(xs, axis=0).astype(BF16), w_ref[c])
        for g in range(N_KV):
            o_ref[0, :, c * 1024 + g * 128:c * 1024 + (g + 1) * 128] = r[g * nh:(g + 1) * nh, 0:128]
            o_ref[0, :, c * 1024 + 512 + g * 128:c * 1024 + 512 + (g + 1) * 128] = r[g * nh:(g + 1) * nh, 128:256]


def _compress(rows3, pt, w2, *, npg):
    nb, npages = pt.shape
    nh = npg * 8
    page_specs = [pl.BlockSpec((PAGE, 8, 128), lambda b, s, pt, k=k: (pt[b, s * npg + k], 0, 0))
                  for k in range(npg)]
    return pl.pallas_call(
        functools.partial(_compress_kernel, npg=npg),
        grid_spec=pltpu.PrefetchScalarGridSpec(
            num_scalar_prefetch=1,
            grid=(nb, npages // npg),
            in_specs=page_specs + [pl.BlockSpec((2, 2048, 256), lambda b, s, pt: (0, 0, 0))],
            out_specs=pl.BlockSpec((1, nh, 2048), lambda b, s, pt: (b, s, 0)),
            scratch_shapes=[pltpu.VMEM((CMP_STRIDE, nh * 8, 128), F32)]),
        out_shape=jax.ShapeDtypeStruct((nb, npages * 8, 2048), F32),
        compiler_params=_cp(("parallel", "parallel")),
        name="compress",
    )(pt, *([rows3] * npg), w2)


def _posterm_kernel(pos_ref, w_ref, o_ref):
    o_ref[...] = jnp.zeros_like(o_ref)
    for c in range(2):
        x = pos_ref[c]
        w = w_ref[c]
        xh = x.astype(BF16)
        xl = (x - xh.astype(F32)).astype(BF16)
        wh = w.astype(BF16)
        wl = (w - wh.astype(F32)).astype(BF16)
        r = _dot(xh, wh) + _dot(xh, wl) + _dot(xl, wh)
        o_ref[c:c + 1, :] = r[0:1, :]


def _posterm(pos8, w2):
    return pl.pallas_call(
        _posterm_kernel,
        out_shape=jax.ShapeDtypeStruct((8, HEAD_DIM), F32),
        compiler_params=pltpu.CompilerParams(vmem_limit_bytes=VMEM_LIMIT),
        name="posterm",
    )(pos8, w2)


def _nsa_kernel(q_ref, gate_ref, ksel_ref, vsel_ref, kwin_ref, vwin_ref,
                ak_ref, bk_ref, av_ref, bv_ref, pos_ref, ovt_ref, e_ref,
                y_ref, kcat_ref, cmpk_ref, cmpv_ref, bias_ref, *, tq, ck):
    qi = pl.program_id(2)
    t0 = pl.multiple_of(qi * tq, tq)
    n_half = ak_ref.shape[1]
    heads = lambda a: [a[r * tq:(r + 1) * tq] for r in range(GROUP)]

    @pl.when(qi == 0)
    def _():
        kcat_ref[:, 0:HEAD_DIM] = ksel_ref[...]
        kcat_ref[:, HEAD_DIM:2 * HEAD_DIM] = e_ref[...]
        cmpk_ref[...] = (ak_ref[0] + pltpu.roll(bk_ref[0], n_half - 1, 0) + pos_ref[0:1, :]).astype(BF16)
        cmpv_ref[...] = (av_ref[0] + pltpu.roll(bv_ref[0], n_half - 1, 0) + pos_ref[1:2, :]).astype(BF16)

    q = q_ref[...] * SCALE
    qs = jnp.concatenate([q[:, r * HEAD_DIM:(r + 1) * HEAD_DIM] for r in range(GROUP)], axis=0).astype(BF16)
    tcol = t0 + lax.broadcasted_iota(jnp.int32, (tq, 1), 0)

    n_l = lax.broadcasted_iota(jnp.int32, (1, n_half), 1)
    cvalid = (n_l * CMP_STRIDE + (CMP_LEN - 1) <= tcol) & (n_l < n_half - 1)
    cbias = jnp.where(cvalid, 0.0, NEG)
    ckeep = jnp.where(cvalid, 1.0, 0.0)
    s = _dot_nt(qs, cmpk_ref[...])
    s = jnp.concatenate([sh + cbias for sh in heads(s)], axis=0)
    e = jnp.exp(s - jnp.max(s, axis=-1, keepdims=True))
    e = jnp.concatenate([eh * ckeep for eh in heads(e)], axis=0)
    l = jnp.sum(e, axis=-1, keepdims=True)
    p = e / jnp.where(l > 0.0, l, 1.0)
    o_cmp = _dot(p.astype(BF16), cmpv_ref[...])

    wlen = WINDOW + tq
    ws = pl.multiple_of(jnp.maximum(t0 - WINDOW, 0), tq)
    kpos = ws + lax.broadcasted_iota(jnp.int32, (1, wlen), 1)
    wbias = jnp.where((kpos <= tcol) & (kpos > tcol - WINDOW), 0.0, NEG)
    sw = _dot_nt(qs, kwin_ref[pl.ds(ws, wlen), :])
    sw = jnp.concatenate([sh + wbias for sh in heads(sw)], axis=0)
    ew = jnp.exp(sw - jnp.max(_fold(sw, jnp.maximum), axis=-1, keepdims=True))
    pw = ew / jnp.sum(_fold(ew, jnp.add), axis=-1, keepdims=True)
    o_win = _dot(pw.astype(BF16), vwin_ref[pl.ds(ws, wlen), :])

    jj = lax.broadcasted_iota(jnp.int32, (32, tq), 0)
    tl = t0 + lax.broadcasted_iota(jnp.int32, (32, tq), 1)
    elig = jj * SEL_BLOCK <= tl

    def store_bias(sel_mask):
        bias_t = jnp.concatenate([jnp.where(sel_mask, 0.0, -FORCE), jnp.zeros((96, tq), F32)], axis=0)
        bias_ref[...] = bias_t.T.astype(BF16)

    @pl.when(t0 + tq <= TOPK * SEL_BLOCK)
    def _():
        store_bias(elig)

    @pl.when(t0 + tq > TOPK * SEL_BLOCK)
    def _():
        ps = p[0:tq] + p[tq:2 * tq] + p[2 * tq:3 * tq]
        ph = ps.astype(BF16)
        plo = (ps - ph.astype(F32)).astype(BF16)
        imp = (_dot_nt(ovt_ref[...], ph) + _dot_nt(ovt_ref[...], plo))[0:32, :]
        cur = lax.shift_right_logical(tl, 6)
        forced = (jj == 0) | (jj == cur) | (jj == cur - 1)
        score = jnp.where(forced, FORCE, jnp.where(elig, imp, -FORCE))
        sel = jnp.zeros((32, tq), F32)
        for k in range(32):
            row = score[k:k + 1, :]
            beats = (score > row) | ((score == row) & (jj < k))
            cnt = jnp.sum(beats.astype(F32), axis=0, keepdims=True)
            sel = jnp.where(jj == k, (cnt < float(TOPK)).astype(F32), sel)
        store_bias(sel > 0.0)

    qcat = jnp.concatenate([qs, jnp.concatenate([bias_ref[...]] * GROUP, axis=0)], axis=1)
    tcol3 = jnp.concatenate([tcol] * GROUP, axis=0)

    def body(c, carry):
        m, l, acc = carry
        k0 = pl.multiple_of(c * ck, ck)
        sc = _dot_nt(qcat, kcat_ref[pl.ds(k0, ck), :])
        kpos = k0 + lax.broadcasted_iota(jnp.int32, (1, ck), 1)
        sc = jnp.where(kpos <= tcol3, sc, NEG)
        mn = jnp.maximum(m, jnp.max(sc, axis=-1, keepdims=True))
        a = jnp.exp(m - mn)
        pr = jnp.exp(sc - mn)
        l = a * l + jnp.sum(pr, axis=-1, keepdims=True)
        acc = a * acc + _dot(pr.astype(BF16), vsel_ref[pl.ds(k0, ck), :])
        return mn, l, acc

    nch = (t0 + tq + ck - 1) // ck
    m0 = jnp.full((GROUP * tq, 1), NEG, F32)
    l0 = jnp.zeros((GROUP * tq, 1), F32)
    a0 = jnp.zeros((GROUP * tq, HEAD_DIM), F32)
    _, l, acc = lax.fori_loop(0, nch, body, (m0, l0, a0))
    o_sel = acc / l

    gs = _sigmoid(gate_ref[...])
    for r in range(GROUP):
        rs = slice(r * tq, (r + 1) * tq)
        o = (gs[:, r:r + 1] * o_cmp[rs] + gs[:, GROUP + r:GROUP + r + 1] * o_sel[rs]
             + gs[:, 2 * GROUP + r:2 * GROUP + r + 1] * o_win[rs])
        y_ref[:, r * HEAD_DIM:(r + 1) * HEAD_DIM] = o.astype(y_ref.dtype)


def _nsa_prompt(pb, kvb_rows, kvb_win, ab, pos8, ovt, emat, *, nb, t, tq, ck):
    nq = t // tq
    gw = GROUP * HEAD_DIM
    n_half = ab.shape[1]
    return pl.pallas_call(
        functools.partial(_nsa_kernel, tq=tq, ck=ck),
        grid=(nb, N_KV, nq),
        in_specs=[pl.BlockSpec((tq, gw), lambda b, g, i: (b * nq + i, g)),
                  pl.BlockSpec((tq, 128), lambda b, g, i: (b * nq + i, D_A // 128 + g)),
                  pl.BlockSpec((t, 128), lambda b, g, i: (b, 8 + g)),
                  pl.BlockSpec((t, 128), lambda b, g, i: (b, 12 + g)),
                  pl.BlockSpec((t, 128), lambda b, g, i: (b, g)),
                  pl.BlockSpec((t, 128), lambda b, g, i: (b, 4 + g)),
                  pl.BlockSpec((1, n_half, 128), lambda b, g, i: (b, 0, g)),
                  pl.BlockSpec((1, n_half, 128), lambda b, g, i: (b, 0, 4 + g)),
                  pl.BlockSpec((1, n_half, 128), lambda b, g, i: (b, 0, 8 + g)),
                  pl.BlockSpec((1, n_half, 128), lambda b, g, i: (b, 0, 12 + g)),
                  pl.BlockSpec((8, 128), lambda b, g, i: (0, 0)),
                  pl.BlockSpec((128, 128), lambda b, g, i: (0, 0)),
                  pl.BlockSpec((t, 128), lambda b, g, i: (0, 0))],
        out_specs=pl.BlockSpec((tq, gw), lambda b, g, i: (b * nq + i, g)),
        out_shape=jax.ShapeDtypeStruct((nb * t, D_A), BF16),
        scratch_shapes=[pltpu.VMEM((t, 2 * HEAD_DIM), BF16),
                        pltpu.VMEM((n_half, HEAD_DIM), BF16),
                        pltpu.VMEM((n_half, HEAD_DIM), BF16),
                        pltpu.VMEM((tq, 128), BF16)],
        compiler_params=_cp(("parallel", "parallel", "arbitrary")),
        name="nsa_prompt",
    )(pb, pb, kvb_rows, kvb_rows, kvb_win, kvb_win, ab, ab, ab, ab, pos8, ovt, emat)


def _dec1_kernel(q_ref, ab_ref, pos_ref, ov_ref, cwin_ref, knew_ref, vnew_ref,
                 ocmp_ref, owin_ref, idx_ref, *, n_slc):
    n_half = ab_ref.shape[1]
    nlane = ov_ref.shape[1]
    lane = lax.broadcasted_iota(jnp.int32, (8, nlane), 1)
    row8 = lax.broadcasted_iota(jnp.int32, (8, nlane), 0)
    imp_all = jnp.zeros((8, nlane), F32)
    for g in range(N_KV):
        sl = slice(g * HEAD_DIM, (g + 1) * HEAD_DIM)
        qg = (q_ref[0, g] * SCALE).astype(BF16)
        cmpk = (ab_ref[0, :, sl] + pltpu.roll(ab_ref[0, :, 512 + g * 128:512 + (g + 1) * 128], n_half - 1, 0)
                + pos_ref[0:1, :]).astype(BF16)
        cmpv = (ab_ref[0, :, 1024 + g * 128:1024 + (g + 1) * 128]
                + pltpu.roll(ab_ref[0, :, 1536 + g * 128:1536 + (g + 1) * 128], n_half - 1, 0)
                + pos_ref[1:2, :]).astype(BF16)
        n_l = lax.broadcasted_iota(jnp.int32, (1, n_half), 1)
        cm = n_l < n_half - 1
        s = jnp.where(cm, _dot_nt(qg, cmpk), NEG)
        e = jnp.where(cm, jnp.exp(s - jnp.max(s, axis=-1, keepdims=True)), 0.0)
        p = e / jnp.sum(e, axis=-1, keepdims=True)
        ocmp_ref[0, g] = _dot(p.astype(BF16), cmpv)
        ps = p[0:1] + p[1:2] + p[2:3]
        ps8 = jnp.broadcast_to(ps, (8, n_half))
        ph = ps8.astype(BF16)
        plo = (ps8 - ph.astype(F32)).astype(BF16)
        imp = _dot(ph, ov_ref[...]) + _dot(plo, ov_ref[...])
        imp_all = jnp.where(row8 == g, imp, imp_all)

        kc = cwin_ref[0, :, sl].astype(BF16)
        vc = cwin_ref[0, :, 512 + g * 128:512 + (g + 1) * 128].astype(BF16)
        kn = knew_ref[0, g][0:1, :].astype(BF16).astype(F32)
        vn = vnew_ref[0, g][0:1, :].astype(BF16).astype(F32)
        w_l = lax.broadcasted_iota(jnp.int32, (1, kc.shape[0]), 1)
        sc = jnp.where(w_l >= 1, _dot_nt(qg, kc), NEG)
        sn = jnp.sum(qg.astype(F32) * kn, axis=-1, keepdims=True)
        m = jnp.maximum(jnp.max(sc, axis=-1, keepdims=True), sn)
        ec = jnp.exp(sc - m)
        en = jnp.exp(sn - m)
        lw = jnp.sum(ec, axis=-1, keepdims=True) + en
        owin_ref[0, g] = _dot((ec / lw).astype(BF16), vc) + (en / lw).astype(BF16).astype(F32) * vn

    cur = n_slc - 1
    forced = (lane == 0) | (lane == cur) | (lane == cur - 1)
    score = jnp.where(forced, FORCE, jnp.where(lane < n_slc, imp_all, -3.0 * FORCE))
    idx = jnp.zeros((8, 128), F32)
    lane128 = lax.broadcasted_iota(jnp.int32, (8, 128), 1)
    lane_f = lane.astype(F32)
    for it in range(TOPK):
        m = jnp.max(score, axis=-1, keepdims=True)
        ix = jnp.min(jnp.where(score == m, lane_f, 1e6), axis=-1, keepdims=True)
        idx = jnp.where(lane128 == it, ix, idx)
        score = jnp.where(lane_f == ix, -4.0 * FORCE, score)
    idx_ref[0] = idx.astype(jnp.int32)


def _dec1(q4, ab, pos8, ov, cwin, knew4, vnew4, *, n_slc):
    nb = q4.shape[0]
    n_half = ab.shape[1]
    hg = pl.BlockSpec((1, N_KV, 8, HEAD_DIM), lambda b: (b, 0, 0, 0))
    return pl.pallas_call(
        functools.partial(_dec1_kernel, n_slc=n_slc),
        grid=(nb,),
        in_specs=[hg,
                  pl.BlockSpec((1, n_half, 2048), lambda b: (b, 0, 0)),
                  pl.BlockSpec((8, 128), lambda b: (0, 0)),
                  pl.BlockSpec(ov.shape, lambda b: (0, 0)),
                  pl.BlockSpec((1, WINDOW, 1024), lambda b: (b, 0, 0)),
                  hg, hg],
        out_specs=[hg, hg, pl.BlockSpec((1, 8, 128), lambda b: (b, 0, 0))],
        out_shape=[jax.ShapeDtypeStruct((nb, N_KV, 8, HEAD_DIM), F32),
                   jax.ShapeDtypeStruct((nb, N_KV, 8, HEAD_DIM), F32),
                   jax.ShapeDtypeStruct((nb, 8, 128), jnp.int32)],
        compiler_params=_cp(("parallel",)),
        name="dec_cmp_win",
    )(q4, ab, pos8, ov, cwin, knew4, vnew4)


def _dec2_kernel(pt_ref, si_ref, *refs, new_blk):
    del pt_ref
    kv_refs = refs[:TOPK]
    q_ref, knew_ref, vnew_ref, ocmp_ref, owin_ref, gate_ref, y_ref = refs[TOPK:]
    b = pl.program_id(0)
    g = pl.program_id(1)
    qg = (q_ref[0, 0] * SCALE).astype(BF16)
    row = lax.broadcasted_iota(jnp.int32, (SEL_BLOCK, HEAD_DIM), 0)
    lane = lax.broadcasted_iota(jnp.int32, (1, TOPK * SEL_BLOCK), 1)
    kn = knew_ref[0, 0][0:1, :]
    vn = vnew_ref[0, 0][0:1, :]
    ks, vs = [], []
    masked = jnp.zeros((1, TOPK * SEL_BLOCK), jnp.bool_)
    lane_blk = lax.shift_right_logical(lane, 6)
    lane_off = lane & (SEL_BLOCK - 1)
    for kk in range(TOPK):
        is_new = si_ref[(b * N_KV + g) * TOPK + kk] == new_blk
        sub = row == jnp.where(is_new, 0, -1)
        k_blk = kv_refs[kk][pl.ds(2 * N_KV + g, SEL_BLOCK, stride=16), :]
        v_blk = kv_refs[kk][pl.ds(3 * N_KV + g, SEL_BLOCK, stride=16), :]
        ks.append(jnp.where(sub, kn, k_blk).astype(BF16))
        vs.append(jnp.where(sub, vn, v_blk).astype(BF16))
        masked = masked | ((lane_blk == jnp.where(is_new, kk, -1)) & (lane_off > 0))
    kall = jnp.concatenate(ks, axis=0)
    vall = jnp.concatenate(vs, axis=0)
    s = jnp.where(masked, NEG, _dot_nt(qg, kall))
    e = jnp.exp(s - jnp.max(s, axis=-1, keepdims=True))
    p = e / jnp.sum(e, axis=-1, keepdims=True)
    o_sel = _dot(p.astype(BF16), vall)
    gs = _sigmoid(gate_ref[0, 0])
    y_ref[0, 0] = gs[:, 0:1] * ocmp_ref[0, 0] + gs[:, 1:2] * o_sel + gs[:, 2:3] * owin_ref[0, 0]


def _dec2(pt, sidx, cache2, q4, knew4, vnew4, ocmp, owin, gate4, *, new_blk):
    nb = q4.shape[0]
    last = new_blk - 1

    def half_page(b, g, pt, si, kk):
        j = jnp.minimum(si[(b * N_KV + g) * TOPK + kk], last)
        return pt[b, lax.shift_right_logical(j, 1)] * 2 + (j & 1)

    kv_specs = [pl.BlockSpec((SEL_BLOCK * 16, 128), lambda b, g, pt, si, kk=kk: (half_page(b, g, pt, si, kk), 0))
                for kk in range(TOPK)]
    hg = pl.BlockSpec((1, 1, 8, HEAD_DIM), lambda b, g, pt, si: (b, g, 0, 0))
    return pl.pallas_call(
        functools.partial(_dec2_kernel, new_blk=new_blk),
        grid_spec=pltpu.PrefetchScalarGridSpec(
            num_scalar_prefetch=2,
            grid=(nb, N_KV),
            in_specs=kv_specs + [hg] * 6,
            out_specs=hg),
        out_shape=jax.ShapeDtypeStruct((nb, N_KV, 8, HEAD_DIM), F32),
        compiler_params=_cp(("parallel", "parallel")),
        name="dec_sel",
    )(pt, sidx, *([cache2] * TOPK), q4, knew4, vnew4, ocmp, owin, gate4)


def _pad_rows(a, rows):
    return jnp.pad(a, ((0, rows - a.shape[0]),) + ((0, 0),) * (a.ndim - 1))


def _to_hg(a2d, nb):
    g = a2d.shape[1] // (N_KV * HEAD_DIM)
    a = a2d.reshape(nb, N_KV, g, HEAD_DIM)
    return jnp.pad(a, ((0, 0), (0, 0), (0, 8 - g), (0, 0)))


def kernel(x_prompt, x_sample, cache_kv, cache_win, cache_mem, state_conv_mix, state_conv_ffn, page_table,
           mem_prompt, ln_g, ln_b, w_in_a, conv_a_w, w_in_b, w_o, w_mem_kv, w_up, ffn_conv_w, w_down,
           w_kv_shared, cmp_pos, w_cmp):
    nb, t = x_prompt.shape[:2]
    db = x_sample.shape[0]
    m_p = nb * t
    ms = 16
    tm = 512
    tpb = t // tm
    past = page_table.shape[1] * cache_kv.shape[1]

    w_in_a_b = w_in_a.astype(BF16)
    hq = N_HEADS * HEAD_DIM
    hgate = hq + 3 * N_HEADS
    wgate = w_in_b[:, :, hq:hgate].reshape(2, D_MODEL, 3, N_KV, GROUP)
    wgate = jnp.transpose(wgate, (0, 1, 3, 2, 4)).reshape(2, D_MODEL, N_KV, 3 * GROUP)
    wgate = jnp.pad(wgate, ((0, 0), (0, 0), (0, 0), (0, 128 - 3 * GROUP))).reshape(2, D_MODEL, N_KV * 128)
    w_qg_b = jnp.concatenate([w_in_b[:, :, :hq], wgate], axis=-1).astype(BF16)
    w_qm_b = jnp.concatenate([w_in_a[:, :, 3 * D_A:], w_in_b[:, :, hgate:]], axis=0).astype(BF16)
    w_o_b = w_o.astype(BF16)
    padf = D_FF_PAD - D_FF
    w_up_z = jnp.pad(w_up[:, :, :D_FF], ((0, 0), (0, 0), (0, padf))).astype(BF16)
    w_up_g = jnp.pad(w_up[:, :, D_FF:], ((0, 0), (0, 0), (0, padf))).astype(BF16)
    w_down_b = jnp.pad(w_down, ((0, 0), (0, padf), (0, 0))).astype(BF16)
    ffn_cw = jnp.pad(ffn_conv_w, ((0, 0), (0, 0), (0, padf)))
    w_kv_b = w_kv_shared.astype(BF16)
    w_mem_b = w_mem_kv.astype(BF16)
    w_cmp2 = jnp.concatenate([w_cmp[:, :CMP_STRIDE].reshape(2, 2048, 128),
                              w_cmp[:, CMP_STRIDE:].reshape(2, 2048, 128)], axis=-1).astype(BF16)
    pos8 = _posterm(jnp.broadcast_to(cmp_pos.reshape(2, 1, CMP_LEN * HEAD_DIM), (2, 8, CMP_LEN * HEAD_DIM)),
                    w_cmp.reshape(2, CMP_LEN * HEAD_DIM, HEAD_DIM))
    g2 = ln_g.reshape(DEPTH, 2, 1, D_MODEL)
    b2 = ln_b.reshape(DEPTH, 2, 1, D_MODEL)

    def overlap(n_cmp, n_lanes_j, n_rows):
        n = jnp.arange(n_rows)[:, None]
        j = jnp.arange(n_lanes_j)[None, :]
        return ((n >= 4 * j - 1) & (n <= 4 * j + 3) & (n < n_cmp)).astype(BF16)

    ovt_p = overlap(t // CMP_STRIDE - 1, 128, 128).T
    emat = (jnp.arange(t)[:, None] // SEL_BLOCK == jnp.arange(128)[None, :]).astype(BF16)
    n_half_s = past // CMP_STRIDE
    n_slc_s = past // SEL_BLOCK + 1
    ov_s = overlap(n_half_s - 1, 384, n_half_s)

    x = x_prompt.reshape(m_p, D_MODEL)
    mem_x = mem_prompt.reshape(nb * N_MEM, D_MODEL)
    mem_kv = [_mm(mem_x, w_mem_b, layer=l, tm=512, tn=2 * MEM_W, name="mm_memkv").reshape(nb, N_MEM, 2 * MEM_W)
              for l in range(DEPTH)]
    zeros_mix = jnp.zeros((nb, 8, D_A), F32)
    zeros_ffn = jnp.zeros((nb, 8, D_FF_PAD), F32)
    conv_p, ffn_p = [], []

    def tail(x, l, y, memkv_l, seq_prev8):
        mo = _memattn(x.reshape(nb, t, D_MODEL), w_qm_b, memkv_l, layer=l, tt=tm, out_dtype=BF16).reshape(m_p, MEM_W)
        x = _oproj(x, y, mo, w_o_b, g2[l, 0], b2[l, 0], layer=l, tm=tm)
        x, st = _ffn_seq(x, w_up_z, w_up_g, ffn_cw, w_down_b, seq_prev8, g2[l, 1], b2[l, 1],
                         layer=l, tm=tm, tf=512, tpb=tpb)
        return x, st

    for l in range(N_A):
        y, st = _amix_seq(x, w_in_a_b, conv_a_w, zeros_mix, layer=l, tm=tm, tn=512, tpb=tpb)
        conv_p.append(st[tpb - 1::tpb, 6:8])
        x, fs = tail(x, l, y, mem_kv[l], zeros_ffn)
        ffn_p.append(fs[tpb - 1::tpb, 6:8, :D_FF])

    kv_tr, kvb_rows = _kvrows(x, w_kv_b, tm=tm, tn=2048)
    kv_win, kvb_win = _mm(x, w_kv_b, tm=tm, tn=1024, n=1024, col0=2, out_dtypes=(F32, BF16), name="mm_kvwin")
    pt_p = jnp.arange(m_p // PAGE, dtype=jnp.int32).reshape(nb, t // PAGE)
    ab_p = _compress(kv_tr.reshape(m_p, 16, HEAD_DIM), pt_p, w_cmp2, npg=t // PAGE)

    for j in range(DEPTH - N_A):
        l = N_A + j
        pb = _mm(x, w_qg_b, layer=j, tm=tm, tn=2048, name="mm_qgate")
        y = _nsa_prompt(pb, kvb_rows, kvb_win, ab_p, pos8, ovt_p, emat, nb=nb, t=t, tq=256, ck=512)
        x, fs = tail(x, l, y, mem_kv[l], zeros_ffn)
        ffn_p.append(fs[tpb - 1::tpb, 6:8, :D_FF])
    y_prompt = x.reshape(nb, t, D_MODEL)

    xs = _pad_rows(x_sample.reshape(db, D_MODEL), ms)
    cmem = cache_mem.reshape(DEPTH, db, N_MEM, 2 * MEM_W)
    conv_s, ffn_s = [], []

    def tail_s(xs, l, y):
        xrep = jnp.broadcast_to(xs[:db, None, :], (db, 8, D_MODEL))
        mo = _pad_rows(_memattn(xrep, w_qm_b, cmem[l], layer=l, tt=8, out_dtype=F32)[:, 0], ms)
        xs = _oproj(xs, y, mo, w_o_b, g2[l, 0], b2[l, 0], layer=l, tm=ms)
        p0 = _pad_rows(jnp.pad(state_conv_ffn[l, :, 0], ((0, 0), (0, padf))), ms)
        p1 = _pad_rows(jnp.pad(state_conv_ffn[l, :, 1], ((0, 0), (0, padf))), ms)
        xs, uz = _ffn_tok(xs, w_up_z, w_up_g, ffn_cw, w_down_b, p0, p1, g2[l, 1], b2[l, 1], layer=l, tf=512)
        ffn_s.append(jnp.stack([state_conv_ffn[l, :, 1], uz[:db, :D_FF]], axis=1))
        return xs

    for l in range(N_A):
        p0 = _pad_rows(state_conv_mix[l, :, 0], ms)
        p1 = _pad_rows(state_conv_mix[l, :, 1], ms)
        y, cu = _amix_tok(xs, w_in_a_b, conv_a_w, p0, p1, layer=l, tn=512)
        conv_s.append(jnp.stack([state_conv_mix[l, :, 1], cu[:db]], axis=1))
        xs = tail_s(xs, l, y)

    kv_s = _mm(xs, w_kv_b, tm=ms, tn=512, name="mm_kv_tok")[:db]
    n_tok = cache_kv.shape[0] * PAGE
    ab_s = _compress(cache_kv.reshape(n_tok, 16, HEAD_DIM), page_table, w_cmp2, npg=16)
    cache2 = cache_kv.reshape(n_tok * 16, HEAD_DIM)
    cwin = cache_win.reshape(db, WINDOW, 1024)
    ksel_new = _to_hg(kv_s[:, 1024:1536], db)
    vsel_new = _to_hg(kv_s[:, 1536:2048], db)
    kwin_new = _to_hg(kv_s[:, 2048:2560], db)
    vwin_new = _to_hg(kv_s[:, 2560:3072], db)

    for j in range(DEPTH - N_A):
        l = N_A + j
        pbs = _mm(xs, w_qg_b, layer=j, tm=ms, tn=512, name="mm_qgate_tok")[:db]
        q4 = _to_hg(pbs[:, :hq], db)
        gate4 = pbs[:, hq:].reshape(db, N_KV, 128)[:, :, :3 * GROUP].reshape(db, N_KV, 3, GROUP)
        gate4 = jnp.pad(gate4.transpose(0, 1, 3, 2), ((0, 0), (0, 0), (0, 8 - GROUP), (0, 128 - 3)))
        ocmp, owin, idx = _dec1(q4, ab_s, pos8, ov_s, cwin, kwin_new, vwin_new, n_slc=n_slc_s)
        sidx = idx[:, :N_KV, :TOPK].reshape(db * N_KV * TOPK)
        y4 = _dec2(page_table, sidx, cache2, q4, ksel_new, vsel_new, ocmp, owin, gate4, new_blk=n_slc_s - 1)
        y = _pad_rows(y4[:, :, :GROUP].reshape(db, D_A), ms).astype(BF16)
        xs = tail_s(xs, l, y)
    y_sample = xs[:db].reshape(db, 1, D_MODEL)

    wb = cache_win.shape[1]
    win_new = kv_s[:, 2048:].reshape(db, 1, 2, N_KV, HEAD_DIM)
    return (y_prompt, y_sample,
            kv_tr.reshape(nb, t, 4, N_KV, HEAD_DIM),
            kv_win.reshape(nb, t, 1024)[:, max(t - WINDOW, 0):].reshape(nb, min(t, WINDOW), 2, N_KV, HEAD_DIM),
            jnp.stack(mem_kv).reshape(DEPTH, nb, N_MEM, 2, 4, HEAD_DIM),
            jnp.stack(conv_p),
            jnp.stack(ffn_p),
            kv_s[:, :2048].reshape(db, 1, 4, N_KV, HEAD_DIM),
            jnp.concatenate([cache_win, win_new], axis=1)[:, -wb:],
            jnp.stack(conv_s),
            jnp.stack(ffn_s))
```

```python
import functools

import jax
import jax.numpy as jnp
from jax import lax
from jax.experimental import pallas as pl
from jax.experimental.pallas import tpu as pltpu

F32 = jnp.float32
BF16 = jnp.bfloat16

D_MODEL = 2048
DEPTH = 4
HEAD_DIM = 128
N_A = 2
D_A = 1536
N_HEADS = 12
N_KV = 4
GROUP = 3
MEM_W = 512
N_MEM = 256
D_FF = 5504
D_FF_PAD = 5632
CMP_STRIDE = 16
CMP_LEN = 32
SEL_BLOCK = 64
TOPK = 16
WINDOW = 512
PAGE = 128
ALPHA = (2 * DEPTH) ** 0.25
LN_EPS = 1e-5
NEG = -1e30
FORCE = 1e9
SCALE = HEAD_DIM ** -0.5
LOG2E = 1.4426950408889634

VMEM_LIMIT = 56 * 1024 * 1024


def _cp(sem):
    return pltpu.CompilerParams(dimension_semantics=sem, vmem_limit_bytes=VMEM_LIMIT)


def _dot(a, b):
    return jnp.dot(a, b, preferred_element_type=F32)


def _wspec(block, index_map, layer=None):
    if layer is None:
        return pl.BlockSpec(block, index_map)
    return pl.BlockSpec((None,) + block, lambda *a: (layer,) + index_map(*a))


def _dot_nt(a, b):
    return lax.dot_general(a, b, (((1,), (1,)), ((), ())), preferred_element_type=F32)


def _layer_norm(v, g, b):
    mu = jnp.mean(v, axis=-1, keepdims=True)
    d = v - mu
    var = jnp.mean(d * d, axis=-1, keepdims=True)
    return d * lax.rsqrt(var + LN_EPS) * g + b


def _sigmoid(x):
    return 1.0 / (1.0 + jnp.exp(-x))


def _fold(x, op):
    r = x[:, 0:128]
    for i in range(1, x.shape[1] // 128):
        r = op(r, x[:, i * 128:(i + 1) * 128])
    return r


def _mm_kernel(x_ref, w_ref, *rest, n_out):
    o_refs = rest[:n_out]
    xb_ref = rest[n_out]

    @pl.when(pl.program_id(1) == 0)
    def _():
        xb_ref[...] = x_ref[...].astype(BF16)

    r = _dot(xb_ref[...], w_ref[...])
    for o in o_refs:
        o[...] = r.astype(o.dtype)


def _mm(x, w, *, tm, tn, n=None, col0=0, layer=None, out_dtypes=(F32,), name="mm"):
    m, k = x.shape
    n = w.shape[-1] if n is None else n
    outs = pl.pallas_call(
        functools.partial(_mm_kernel, n_out=len(out_dtypes)),
        grid=(m // tm, n // tn),
        in_specs=[pl.BlockSpec((tm, k), lambda i, j: (i, 0)),
                  _wspec((k, tn), lambda i, j: (0, col0 + j), layer)],
        out_specs=[pl.BlockSpec((tm, tn), lambda i, j: (i, j)) for _ in out_dtypes],
        out_shape=[jax.ShapeDtypeStruct((m, n), d) for d in out_dtypes],
        scratch_shapes=[pltpu.VMEM((tm, k), BF16)],
        compiler_params=_cp(("parallel", "arbitrary")),
        name=name,
    )(x, w)
    return outs if len(out_dtypes) > 1 else outs[0]


def _kvrows_kernel(x_ref, w_ref, tr_ref, ob_ref, xb_ref):
    j = pl.program_id(1)

    @pl.when(j == 0)
    def _():
        xb_ref[...] = x_ref[...].astype(BF16)

    r = _dot(xb_ref[...], w_ref[...])
    ob_ref[...] = r.astype(BF16)
    tm = r.shape[0]
    for c in range(r.shape[1] // HEAD_DIM):
        tr_ref[pl.ds(j * (r.shape[1] // HEAD_DIM) + c, tm, stride=16), :] = r[:, c * HEAD_DIM:(c + 1) * HEAD_DIM]


def _kvrows(x, w, *, tm, tn):
    m, k = x.shape
    n = 16 * HEAD_DIM
    return pl.pallas_call(
        _kvrows_kernel,
        grid=(m // tm, n // tn),
        in_specs=[pl.BlockSpec((tm, k), lambda i, j: (i, 0)),
                  pl.BlockSpec((k, tn), lambda i, j: (0, j))],
        out_specs=[pl.BlockSpec((tm * 16, HEAD_DIM), lambda i, j: (i, 0)),
                   pl.BlockSpec((tm, tn), lambda i, j: (i, j))],
        out_shape=[jax.ShapeDtypeStruct((m * 16, HEAD_DIM), F32),
                   jax.ShapeDtypeStruct((m, n), BF16)],
        scratch_shapes=[pltpu.VMEM((tm, k), BF16)],
        compiler_params=_cp(("parallel", "arbitrary")),
        name="kvrows",
    )(x, w)


def _conv_rows(v, cw_ref, tail8):
    row = lax.broadcasted_iota(jnp.int32, v.shape, 0)
    t0 = tail8[6:7, :]
    t1 = tail8[7:8, :]
    s1 = jnp.where(row == 0, t1, pltpu.roll(v, 1, 0))
    s2 = jnp.where(row == 0, t0, jnp.where(row == 1, t1, pltpu.roll(v, 2, 0)))
    return cw_ref[0:1, :] * s2 + cw_ref[1:2, :] * s1 + cw_ref[2:3, :] * v


def _amix_seq_kernel(x_ref, wu_ref, wb_ref, wc_ref, cw_ref, prev_ref, y_ref, st_ref,
                     xb_ref, carry_ref, *, tpb):
    i = pl.program_id(0)
    j = pl.program_id(1)

    @pl.when(j == 0)
    def _():
        xb_ref[...] = x_ref[...].astype(BF16)

    xb = xb_ref[...]
    cu = _dot(xb, wc_ref[...]) * _dot(xb, wu_ref[...])
    tail8 = jnp.where((i % tpb) == 0, prev_ref[0], carry_ref[j])
    conv = _conv_rows(cu, cw_ref, tail8)
    y_ref[...] = (_dot(xb, wb_ref[...]) * conv).astype(y_ref.dtype)
    last8 = cu[cu.shape[0] - 8:, :]
    carry_ref[j] = last8
    st_ref[0] = last8


def _amix_seq(x, w_in, cw, prev8, *, layer, tm, tn, tpb):
    m = x.shape[0]
    nj = D_A // tn
    return pl.pallas_call(
        functools.partial(_amix_seq_kernel, tpb=tpb),
        grid=(m // tm, nj),
        in_specs=[pl.BlockSpec((tm, D_MODEL), lambda i, j: (i, 0)),
                  _wspec((D_MODEL, tn), lambda i, j: (0, j), layer),
                  _wspec((D_MODEL, tn), lambda i, j: (0, nj + j), layer),
                  _wspec((D_MODEL, tn), lambda i, j: (0, 2 * nj + j), layer),
                  _wspec((3, tn), lambda i, j: (0, j), layer),
                  pl.BlockSpec((1, 8, tn), lambda i, j: (i // tpb, 0, j))],
        out_specs=[pl.BlockSpec((tm, tn), lambda i, j: (i, j)),
                   pl.BlockSpec((1, 8, tn), lambda i, j: (i, 0, j))],
        out_shape=[jax.ShapeDtypeStruct((m, D_A), BF16),
                   jax.ShapeDtypeStruct((m // tm, 8, D_A), F32)],
        scratch_shapes=[pltpu.VMEM((tm, D_MODEL), BF16),
                        pltpu.VMEM((nj, 8, tn), F32)],
        compiler_params=_cp(("arbitrary", "arbitrary")),
        name="amix_seq",
    )(x, w_in, w_in, w_in, cw, prev8)


def _amix_tok_kernel(x_ref, wu_ref, wb_ref, wc_ref, cw_ref, p0_ref, p1_ref, y_ref, cu_ref):
    xb = x_ref[...].astype(BF16)
    cu = _dot(xb, wc_ref[...]) * _dot(xb, wu_ref[...])
    conv = cw_ref[0:1, :] * p0_ref[...] + cw_ref[1:2, :] * p1_ref[...] + cw_ref[2:3, :] * cu
    y_ref[...] = (_dot(xb, wb_ref[...]) * conv).astype(y_ref.dtype)
    cu_ref[...] = cu


def _amix_tok(x, w_in, cw, prev, *, layer, tn):
    m = x.shape[0]
    nj = D_A // tn
    return pl.pallas_call(
        _amix_tok_kernel,
        grid=(nj,),
        in_specs=[pl.BlockSpec((m, D_MODEL), lambda j: (0, 0)),
                  _wspec((D_MODEL, tn), lambda j: (0, j), layer),
                  _wspec((D_MODEL, tn), lambda j: (0, nj + j), layer),
                  _wspec((D_MODEL, tn), lambda j: (0, 2 * nj + j), layer),
                  _wspec((3, tn), lambda j: (0, j), layer),
                  pl.BlockSpec((None, None, m, tn), lambda j: (layer, 0, 0, j)),
                  pl.BlockSpec((None, None, m, tn), lambda j: (layer, 1, 0, j))],
        out_specs=[pl.BlockSpec((m, tn), lambda j: (0, j)),
                   pl.BlockSpec((m, tn), lambda j: (0, j))],
        out_shape=[jax.ShapeDtypeStruct((m, D_A), BF16),
                   jax.ShapeDtypeStruct((m, D_A), F32)],
        compiler_params=_cp(("parallel",)),
        name="amix_tok",
    )(x, w_in, w_in, w_in, cw, prev, prev)


def _memattn_body(x_ref, wq_ref, o_ref, key, val):
    xb = x_ref[0].astype(BF16)
    qm = _dot(xb, wq_ref[...]) * SCALE
    for h in range(4):
        sl = slice(h * HEAD_DIM, (h + 1) * HEAD_DIM)
        q = qm[:, sl].astype(BF16)
        s = _dot_nt(q, key(h).astype(BF16))
        e = jnp.exp(s - jnp.max(s, axis=-1, keepdims=True))
        p = e / jnp.sum(e, axis=-1, keepdims=True)
        o_ref[0, :, sl] = _dot(p.astype(BF16), val(h).astype(BF16)).astype(o_ref.dtype)


def _memattn_kernel(x_ref, wq_ref, mk_ref, mv_ref, o_ref):
    _memattn_body(x_ref, wq_ref, o_ref,
                  lambda h: mk_ref[0, :, h * HEAD_DIM:(h + 1) * HEAD_DIM],
                  lambda h: mv_ref[0, :, h * HEAD_DIM:(h + 1) * HEAD_DIM])


def _memattn_rows_kernel(x_ref, wq_ref, mkv_ref, o_ref):
    _memattn_body(x_ref, wq_ref, o_ref,
                  lambda h: mkv_ref[pl.ds(h, N_MEM, stride=8), :],
                  lambda h: mkv_ref[pl.ds(4 + h, N_MEM, stride=8), :])


def _memattn_rows(x3, wq, memrows, *, layer, tt, out_dtype):
    nb, t, _ = x3.shape
    return pl.pallas_call(
        _memattn_rows_kernel,
        grid=(nb, t // tt),
        in_specs=[pl.BlockSpec((1, tt, D_MODEL), lambda b, i: (b, i, 0)),
                  _wspec((D_MODEL, MEM_W), lambda b, i: (0, 0), layer),
                  pl.BlockSpec((N_MEM * 8, HEAD_DIM), lambda b, i: (layer * nb + b, 0))],
        out_specs=pl.BlockSpec((1, tt, MEM_W), lambda b, i: (b, i, 0)),
        out_shape=jax.ShapeDtypeStruct((nb, t, MEM_W), out_dtype),
        compiler_params=_cp(("parallel", "parallel")),
        name="memattn_rows",
    )(x3, wq, memrows)


def _memattn(x3, wq, memkv, *, layer, tt, out_dtype):
    nb, t, _ = x3.shape
    return pl.pallas_call(
        _memattn_kernel,
        grid=(nb, t // tt),
        in_specs=[pl.BlockSpec((1, tt, D_MODEL), lambda b, i: (b, i, 0)),
                  _wspec((D_MODEL, MEM_W), lambda b, i: (0, 0), layer),
                  pl.BlockSpec((1, N_MEM, MEM_W), lambda b, i: (b, 0, 0)),
                  pl.BlockSpec((1, N_MEM, MEM_W), lambda b, i: (b, 0, 1))],
        out_specs=pl.BlockSpec((1, tt, MEM_W), lambda b, i: (b, i, 0)),
        out_shape=jax.ShapeDtypeStruct((nb, t, MEM_W), out_dtype),
        compiler_params=_cp(("parallel", "parallel")),
        name="memattn",
    )(x3, wq, memkv, memkv)


def _oproj_kernel(x_ref, y_ref, m_ref, wy_ref, wm_ref, g_ref, b_ref, o_ref):
    acc = _dot(y_ref[...].astype(BF16), wy_ref[...]) + _dot(m_ref[...].astype(BF16), wm_ref[...])
    o_ref[...] = _layer_norm(ALPHA * x_ref[...] + acc, g_ref[...], b_ref[...])


def _oproj(x, y, mo, wo, g, b, *, layer, tm):
    gb = 2 * layer
    m = x.shape[0]
    return pl.pallas_call(
        _oproj_kernel,
        grid=(m // tm,),
        in_specs=[pl.BlockSpec((tm, D_MODEL), lambda i: (i, 0)),
                  pl.BlockSpec((tm, D_A), lambda i: (i, 0)),
                  pl.BlockSpec((tm, MEM_W), lambda i: (i, 0)),
                  _wspec((D_A, D_MODEL), lambda i: (0, 0), layer),
                  _wspec((MEM_W, D_MODEL), lambda i: (D_A // MEM_W, 0), layer),
                  _wspec((1, D_MODEL), lambda i: (0, 0), gb),
                  _wspec((1, D_MODEL), lambda i: (0, 0), gb)],
        out_specs=pl.BlockSpec((tm, D_MODEL), lambda i: (i, 0)),
        out_shape=jax.ShapeDtypeStruct((m, D_MODEL), F32),
        compiler_params=_cp(("parallel",)),
        name="oproj_ln",
    )(x, y, mo, wo, wo, g, b)


def _ffn_seq_kernel(x_ref, wz_ref, wg_ref, cw_ref, wd_ref, prev_ref, g_ref, b_ref,
                    o_ref, st_ref, xb_ref, acc_ref, carry_ref, *, tpb):
    i = pl.program_id(0)
    j = pl.program_id(1)

    @pl.when(j == 0)
    def _():
        xb_ref[...] = x_ref[...].astype(BF16)
        acc_ref[...] = jnp.zeros_like(acc_ref)

    xb = xb_ref[...]
    uz = _dot(xb, wz_ref[...])
    tail8 = jnp.where((i % tpb) == 0, prev_ref[0], carry_ref[j])
    z = _conv_rows(uz, cw_ref, tail8)
    h = z * _sigmoid(z) * _dot(xb, wg_ref[...])
    acc_ref[...] += _dot(h.astype(BF16), wd_ref[...])
    last8 = uz[uz.shape[0] - 8:, :]
    carry_ref[j] = last8
    st_ref[0] = last8

    @pl.when(j == pl.num_programs(1) - 1)
    def _():
        o_ref[...] = _layer_norm(ALPHA * x_ref[...] + acc_ref[...], g_ref[...], b_ref[...])


def _ffn_seq(x, wz, wg, cw, wd, prev8, g, b, *, layer, tm, tf, tpb):
    gb = 2 * layer + 1
    m = x.shape[0]
    nj = D_FF_PAD // tf
    return pl.pallas_call(
        functools.partial(_ffn_seq_kernel, tpb=tpb),
        grid=(m // tm, nj),
        in_specs=[pl.BlockSpec((tm, D_MODEL), lambda i, j: (i, 0)),
                  _wspec((D_MODEL, tf), lambda i, j: (0, j), layer),
                  _wspec((D_MODEL, tf), lambda i, j: (0, j), layer),
                  _wspec((3, tf), lambda i, j: (0, j), layer),
                  _wspec((tf, D_MODEL), lambda i, j: (j, 0), layer),
                  pl.BlockSpec((1, 8, tf), lambda i, j: (i // tpb, 0, j)),
                  _wspec((1, D_MODEL), lambda i, j: (0, 0), gb),
                  _wspec((1, D_MODEL), lambda i, j: (0, 0), gb)],
        out_specs=[pl.BlockSpec((tm, D_MODEL), lambda i, j: (i, 0)),
                   pl.BlockSpec((1, 8, tf), lambda i, j: (i, 0, j))],
        out_shape=[jax.ShapeDtypeStruct((m, D_MODEL), F32),
                   jax.ShapeDtypeStruct((m // tm, 8, D_FF_PAD), F32)],
        scratch_shapes=[pltpu.VMEM((tm, D_MODEL), BF16),
                        pltpu.VMEM((tm, D_MODEL), F32),
                        pltpu.VMEM((nj, 8, tf), F32)],
        compiler_params=_cp(("arbitrary", "arbitrary")),
        name="ffn_seq",
    )(x, wz, wg, cw, wd, prev8, g, b)


def _ffn_tok_kernel(x_ref, wz_ref, wg_ref, cw_ref, wd_ref, p0_ref, p1_ref, g_ref, b_ref,
                    o_ref, uz_ref, acc_ref):
    j = pl.program_id(0)

    @pl.when(j == 0)
    def _():
        acc_ref[...] = jnp.zeros_like(acc_ref)

    xb = x_ref[...].astype(BF16)
    uz = _dot(xb, wz_ref[...])
    z = cw_ref[0:1, :] * p0_ref[...] + cw_ref[1:2, :] * p1_ref[...] + cw_ref[2:3, :] * uz
    h = z * _sigmoid(z) * _dot(xb, wg_ref[...])
    acc_ref[...] += _dot(h.astype(BF16), wd_ref[...])
    uz_ref[...] = uz

    @pl.when(j == pl.num_programs(0) - 1)
    def _():
        o_ref[...] = _layer_norm(ALPHA * x_ref[...] + acc_ref[...], g_ref[...], b_ref[...])


def _ffn_tok(x, wz, wg, cw, wd, prev, g, b, *, layer, tf):
    gb = 2 * layer + 1
    m = x.shape[0]
    nj = D_FF_PAD // tf
    return pl.pallas_call(
        _ffn_tok_kernel,
        grid=(nj,),
        in_specs=[pl.BlockSpec((m, D_MODEL), lambda j: (0, 0)),
                  _wspec((D_MODEL, tf), lambda j: (0, j), layer),
                  _wspec((D_MODEL, tf), lambda j: (0, j), layer),
                  _wspec((3, tf), lambda j: (0, j), layer),
                  _wspec((tf, D_MODEL), lambda j: (j, 0), layer),
                  pl.BlockSpec((None, None, m, tf), lambda j: (layer, 0, 0, j)),
                  pl.BlockSpec((None, None, m, tf), lambda j: (layer, 1, 0, j)),
                  _wspec((1, D_MODEL), lambda j: (0, 0), gb),
                  _wspec((1, D_MODEL), lambda j: (0, 0), gb)],
        out_specs=[pl.BlockSpec((m, D_MODEL), lambda j: (0, 0)),
                   pl.BlockSpec((m, tf), lambda j: (0, j))],
        out_shape=[jax.ShapeDtypeStruct((m, D_MODEL), F32),
                   jax.ShapeDtypeStruct((m, D_FF_PAD), F32)],
        scratch_shapes=[pltpu.VMEM((m, D_MODEL), F32)],
        compiler_params=_cp(("arbitrary",)),
        name="ffn_tok",
    )(x, wz, wg, cw, wd, prev, prev, g, b)


def _compress_kernel(pt_ref, *refs, npg):
    del pt_ref
    page_refs = refs[:npg]
    w_ref, o_ref, x2_ref = refs[npg:]
    nh = npg * 8
    for pg in range(npg):
        for r in range(PAGE):
            n = pg * 8 + r // CMP_STRIDE
            x2_ref[r % CMP_STRIDE, n * 8:n * 8 + 8, :] = page_refs[pg][r]
    for c in range(2):
        xs = [jnp.concatenate([x2_ref[s, pl.ds(c * N_KV + g, nh, stride=8), :] for s in range(CMP_STRIDE)], axis=1)
              for g in range(N_KV)]
        r = _dot(jnp.concatenate(xs, axis=0).astype(BF16), w_ref[c])
        for g in range(N_KV):
            o_ref[0, :, c * 1024 + g * 128:c * 1024 + (g + 1) * 128] = r[g * nh:(g + 1) * nh, 0:128]
            o_ref[0, :, c * 1024 + 512 + g * 128:c * 1024 + 512 + (g + 1) * 128] = r[g * nh:(g + 1) * nh, 128:256]


def _compress(rows3, pt, w2, *, npg):
    nb, npages = pt.shape
    nh = npg * 8
    page_specs = [pl.BlockSpec((PAGE, 8, 128), lambda b, s, pt, k=k: (pt[b, s * npg + k], 0, 0))
                  for k in range(npg)]
    return pl.pallas_call(
        functools.partial(_compress_kernel, npg=npg),
        grid_spec=pltpu.PrefetchScalarGridSpec(
            num_scalar_prefetch=1,
            grid=(nb, npages // npg),
            in_specs=page_specs + [pl.BlockSpec((2, 2048, 256), lambda b, s, pt: (0, 0, 0))],
            out_specs=pl.BlockSpec((1, nh, 2048), lambda b, s, pt: (b, s, 0)),
            scratch_shapes=[pltpu.VMEM((CMP_STRIDE, nh * 8, 128), F32)]),
        out_shape=jax.ShapeDtypeStruct((nb, npages * 8, 2048), F32),
        compiler_params=_cp(("parallel", "parallel")),
        name="compress",
    )(pt, *([rows3] * npg), w2)


def _posterm_kernel(pos_ref, w_ref, o_ref):
    o_ref[...] = jnp.zeros_like(o_ref)
    for c in range(2):
        x = pos_ref[c]
        w = w_ref[c]
        xh = x.astype(BF16)
        xl = (x - xh.astype(F32)).astype(BF16)
        wh = w.astype(BF16)
        wl = (w - wh.astype(F32)).astype(BF16)
        r = _dot(xh, wh) + _dot(xh, wl) + _dot(xl, wh)
        o_ref[c:c + 1, :] = r[0:1, :]


def _posterm(pos8, w2):
    return pl.pallas_call(
        _posterm_kernel,
        out_shape=jax.ShapeDtypeStruct((8, HEAD_DIM), F32),
        compiler_params=pltpu.CompilerParams(vmem_limit_bytes=VMEM_LIMIT),
        name="posterm",
    )(pos8, w2)


def _nsa_kernel(q_ref, gate_ref, ksel_ref, vsel_ref, kwin_ref, vwin_ref,
                ak_ref, bk_ref, av_ref, bv_ref, pos_ref, ovt_ref, e_ref,
                y_ref, kcat_ref, cmpk_ref, cmpv_ref, bias_ref, *, tq, ck):
    qi = pl.program_id(2)
    t0 = pl.multiple_of(qi * tq, tq)
    n_half = ak_ref.shape[1]
    heads = lambda a: [a[r * tq:(r + 1) * tq] for r in range(GROUP)]

    @pl.when(qi == 0)
    def _():
        kcat_ref[:, 0:HEAD_DIM] = ksel_ref[...]
        kcat_ref[:, HEAD_DIM:2 * HEAD_DIM] = e_ref[...]
        cmpk_ref[...] = (ak_ref[0] + pltpu.roll(bk_ref[0], n_half - 1, 0) + pos_ref[0:1, :]).astype(BF16)
        cmpv_ref[...] = (av_ref[0] + pltpu.roll(bv_ref[0], n_half - 1, 0) + pos_ref[1:2, :]).astype(BF16)

    q = q_ref[...] * (SCALE * LOG2E)
    qs = jnp.concatenate([q[:, r * HEAD_DIM:(r + 1) * HEAD_DIM] for r in range(GROUP)], axis=0).astype(BF16)
    tcol = t0 + lax.broadcasted_iota(jnp.int32, (tq, 1), 0)

    n_l = lax.broadcasted_iota(jnp.int32, (1, n_half), 1)
    cvalid = (n_l * CMP_STRIDE + (CMP_LEN - 1) <= tcol) & (n_l < n_half - 1)
    cbias = jnp.where(cvalid, 0.0, NEG)
    ckeep = jnp.where(cvalid, 1.0, 0.0)
    s = _dot_nt(qs, cmpk_ref[...])
    s = jnp.concatenate([sh + cbias for sh in heads(s)], axis=0)
    e = jnp.exp2(s - jnp.max(s, axis=-1, keepdims=True))
    e = jnp.concatenate([eh * ckeep for eh in heads(e)], axis=0)
    l = jnp.sum(e, axis=-1, keepdims=True)
    p = e / jnp.where(l > 0.0, l, 1.0)
    o_cmp = _dot(p.astype(BF16), cmpv_ref[...])

    wlen = WINDOW + tq
    ws = pl.multiple_of(jnp.maximum(t0 - WINDOW, 0), tq)
    kpos = ws + lax.broadcasted_iota(jnp.int32, (1, wlen), 1)
    wbias = jnp.where((kpos <= tcol) & (kpos > tcol - WINDOW), 0.0, NEG)
    sw = _dot_nt(qs, kwin_ref[pl.ds(ws, wlen), :])
    sw = jnp.concatenate([sh + wbias for sh in heads(sw)], axis=0)
    ew = jnp.exp2(sw - jnp.max(_fold(sw, jnp.maximum), axis=-1, keepdims=True))
    pw = ew / jnp.sum(_fold(ew, jnp.add), axis=-1, keepdims=True)
    o_win = _dot(pw.astype(BF16), vwin_ref[pl.ds(ws, wlen), :])

    jj = lax.broadcasted_iota(jnp.int32, (32, tq), 0)
    tl = t0 + lax.broadcasted_iota(jnp.int32, (32, tq), 1)
    elig = jj * SEL_BLOCK <= tl

    def store_bias(sel_mask):
        bias_t = jnp.concatenate([jnp.where(sel_mask, 0.0, -FORCE), jnp.zeros((96, tq), F32)], axis=0)
        bias_ref[...] = bias_t.T.astype(BF16)

    @pl.when(t0 + tq <= TOPK * SEL_BLOCK)
    def _():
        store_bias(elig)

    @pl.when(t0 + tq > TOPK * SEL_BLOCK)
    def _():
        ps = p[0:tq] + p[tq:2 * tq] + p[2 * tq:3 * tq]
        ph = ps.astype(BF16)
        plo = (ps - ph.astype(F32)).astype(BF16)
        imp = (_dot_nt(ovt_ref[...], ph) + _dot_nt(ovt_ref[...], plo))[0:32, :]
        cur = lax.shift_right_logical(tl, 6)
        forced = (jj == 0) | (jj == cur) | (jj == cur - 1)
        score = jnp.where(forced, FORCE, jnp.where(elig, imp, -FORCE))
        sel = jnp.zeros((32, tq), F32)
        for k in range(32):
            row = score[k:k + 1, :]
            beats = (score > row) | ((score == row) & (jj < k))
            cnt = jnp.sum(beats.astype(F32), axis=0, keepdims=True)
            sel = jnp.where(jj == k, (cnt < float(TOPK)).astype(F32), sel)
        store_bias(sel > 0.0)

    qcat = jnp.concatenate([qs, jnp.concatenate([bias_ref[...]] * GROUP, axis=0)], axis=1)
    tcol3 = jnp.concatenate([tcol] * GROUP, axis=0)

    def body(c, carry):
        m, l, acc = carry
        k0 = pl.multiple_of(c * ck, ck)
        sc = _dot_nt(qcat, kcat_ref[pl.ds(k0, ck), :])
        kpos = k0 + lax.broadcasted_iota(jnp.int32, (1, ck), 1)
        sc = jnp.where(kpos <= tcol3, sc, NEG)
        mn = jnp.maximum(m, jnp.max(sc, axis=-1, keepdims=True))
        a = jnp.exp2(m - mn)
        pr = jnp.exp2(sc - mn)
        l = a * l + jnp.sum(pr, axis=-1, keepdims=True)
        acc = a * acc + _dot(pr.astype(BF16), vsel_ref[pl.ds(k0, ck), :])
        return mn, l, acc

    nch = (t0 + tq + ck - 1) // ck
    m0 = jnp.full((GROUP * tq, 1), NEG, F32)
    l0 = jnp.zeros((GROUP * tq, 1), F32)
    a0 = jnp.zeros((GROUP * tq, HEAD_DIM), F32)
    _, l, acc = lax.fori_loop(0, nch, body, (m0, l0, a0))
    o_sel = acc / l

    gs = _sigmoid(gate_ref[...])
    for r in range(GROUP):
        rs = slice(r * tq, (r + 1) * tq)
        o = (gs[:, r:r + 1] * o_cmp[rs] + gs[:, GROUP + r:GROUP + r + 1] * o_sel[rs]
             + gs[:, 2 * GROUP + r:2 * GROUP + r + 1] * o_win[rs])
        y_ref[:, r * HEAD_DIM:(r + 1) * HEAD_DIM] = o.astype(y_ref.dtype)


def _nsa_prompt(pb, kvb_rows, kvb_win, ab, pos8, ovt, emat, *, nb, t, tq, ck):
    nq = t // tq
    gw = GROUP * HEAD_DIM
    n_half = ab.shape[1]
    return pl.pallas_call(
        functools.partial(_nsa_kernel, tq=tq, ck=ck),
        grid=(nb, N_KV, nq),
        in_specs=[pl.BlockSpec((tq, gw), lambda b, g, i: (b * nq + i, g)),
                  pl.BlockSpec((tq, 128), lambda b, g, i: (b * nq + i, D_A // 128 + g)),
                  pl.BlockSpec((t, 128), lambda b, g, i: (b, 8 + g)),
                  pl.BlockSpec((t, 128), lambda b, g, i: (b, 12 + g)),
                  pl.BlockSpec((t, 128), lambda b, g, i: (b, g)),
                  pl.BlockSpec((t, 128), lambda b, g, i: (b, 4 + g)),
                  pl.BlockSpec((1, n_half, 128), lambda b, g, i: (b, 0, g)),
                  pl.BlockSpec((1, n_half, 128), lambda b, g, i: (b, 0, 4 + g)),
                  pl.BlockSpec((1, n_half, 128), lambda b, g, i: (b, 0, 8 + g)),
                  pl.BlockSpec((1, n_half, 128), lambda b, g, i: (b, 0, 12 + g)),
                  pl.BlockSpec((8, 128), lambda b, g, i: (0, 0)),
                  pl.BlockSpec((128, 128), lambda b, g, i: (0, 0)),
                  pl.BlockSpec((t, 128), lambda b, g, i: (0, 0))],
        out_specs=pl.BlockSpec((tq, gw), lambda b, g, i: (b * nq + i, g)),
        out_shape=jax.ShapeDtypeStruct((nb * t, D_A), BF16),
        scratch_shapes=[pltpu.VMEM((t, 2 * HEAD_DIM), BF16),
                        pltpu.VMEM((n_half, HEAD_DIM), BF16),
                        pltpu.VMEM((n_half, HEAD_DIM), BF16),
                        pltpu.VMEM((tq, 128), BF16)],
        compiler_params=_cp(("parallel", "parallel", "arbitrary")),
        name="nsa_prompt",
    )(pb, pb, kvb_rows, kvb_rows, kvb_win, kvb_win, ab, ab, ab, ab, pos8, ovt, emat)


def _dec1_kernel(q_ref, ab_ref, pos_ref, ov_ref, cwin_ref, knew_ref, vnew_ref,
                 ocmp_ref, owin_ref, idx_ref, *, n_slc):
    n_half = ab_ref.shape[1]
    nlane = ov_ref.shape[1]
    lane = lax.broadcasted_iota(jnp.int32, (8, nlane), 1)
    row8 = lax.broadcasted_iota(jnp.int32, (8, nlane), 0)
    imp_all = jnp.zeros((8, nlane), F32)
    for g in range(N_KV):
        sl = slice(g * HEAD_DIM, (g + 1) * HEAD_DIM)
        qg = (q_ref[0, g] * SCALE).astype(BF16)
        cmpk = (ab_ref[0, :, sl] + pltpu.roll(ab_ref[0, :, 512 + g * 128:512 + (g + 1) * 128], n_half - 1, 0)
                + pos_ref[0:1, :]).astype(BF16)
        cmpv = (ab_ref[0, :, 1024 + g * 128:1024 + (g + 1) * 128]
                + pltpu.roll(ab_ref[0, :, 1536 + g * 128:1536 + (g + 1) * 128], n_half - 1, 0)
                + pos_ref[1:2, :]).astype(BF16)
        n_l = lax.broadcasted_iota(jnp.int32, (1, n_half), 1)
        cm = n_l < n_half - 1
        s = jnp.where(cm, _dot_nt(qg, cmpk), NEG)
        e = jnp.where(cm, jnp.exp(s - jnp.max(s, axis=-1, keepdims=True)), 0.0)
        p = e / jnp.sum(e, axis=-1, keepdims=True)
        ocmp_ref[0, g] = _dot(p.astype(BF16), cmpv)
        ps = p[0:1] + p[1:2] + p[2:3]
        ps8 = jnp.broadcast_to(ps, (8, n_half))
        ph = ps8.astype(BF16)
        plo = (ps8 - ph.astype(F32)).astype(BF16)
        imp = _dot(ph, ov_ref[...]) + _dot(plo, ov_ref[...])
        imp_all = jnp.where(row8 == g, imp, imp_all)

        kc = cwin_ref[0, :, sl].astype(BF16)
        vc = cwin_ref[0, :, 512 + g * 128:512 + (g + 1) * 128].astype(BF16)
        kn = knew_ref[0, g][0:1, :].astype(BF16).astype(F32)
        vn = vnew_ref[0, g][0:1, :].astype(BF16).astype(F32)
        w_l = lax.broadcasted_iota(jnp.int32, (1, kc.shape[0]), 1)
        sc = jnp.where(w_l >= 1, _dot_nt(qg, kc), NEG)
        sn = jnp.sum(qg.astype(F32) * kn, axis=-1, keepdims=True)
        m = jnp.maximum(jnp.max(sc, axis=-1, keepdims=True), sn)
        ec = jnp.exp(sc - m)
        en = jnp.exp(sn - m)
        lw = jnp.sum(ec, axis=-1, keepdims=True) + en
        owin_ref[0, g] = _dot((ec / lw).astype(BF16), vc) + (en / lw).astype(BF16).astype(F32) * vn

    cur = n_slc - 1
    forced = (lane == 0) | (lane == cur) | (lane == cur - 1)
    score = jnp.where(forced, FORCE, jnp.where(lane < n_slc, imp_all, -3.0 * FORCE))
    idx = jnp.zeros((8, 128), F32)
    lane128 = lax.broadcasted_iota(jnp.int32, (8, 128), 1)
    lane_f = lane.astype(F32)
    for it in range(TOPK):
        m = jnp.max(score, axis=-1, keepdims=True)
        ix = jnp.min(jnp.where(score == m, lane_f, 1e6), axis=-1, keepdims=True)
        idx = jnp.where(lane128 == it, ix, idx)
        score = jnp.where(lane_f == ix, -4.0 * FORCE, score)
    idx_ref[0] = idx.astype(jnp.int32)


def _dec1(q4, ab, pos8, ov, cwin, knew4, vnew4, *, n_slc):
    nb = q4.shape[0]
    n_half = ab.shape[1]
    hg = pl.BlockSpec((1, N_KV, 8, HEAD_DIM), lambda b: (b, 0, 0, 0))
    return pl.pallas_call(
        functools.partial(_dec1_kernel, n_slc=n_slc),
        grid=(nb,),
        in_specs=[hg,
                  pl.BlockSpec((1, n_half, 2048), lambda b: (b, 0, 0)),
                  pl.BlockSpec((8, 128), lambda b: (0, 0)),
                  pl.BlockSpec(ov.shape, lambda b: (0, 0)),
                  pl.BlockSpec((1, WINDOW, 1024), lambda b: (b, 0, 0)),
                  hg, hg],
        out_specs=[hg, hg, pl.BlockSpec((1, 8, 128), lambda b: (b, 0, 0))],
        out_shape=[jax.ShapeDtypeStruct((nb, N_KV, 8, HEAD_DIM), F32),
                   jax.ShapeDtypeStruct((nb, N_KV, 8, HEAD_DIM), F32),
                   jax.ShapeDtypeStruct((nb, 8, 128), jnp.int32)],
        compiler_params=_cp(("parallel",)),
        name="dec_cmp_win",
    )(q4, ab, pos8, ov, cwin, knew4, vnew4)


def _dec2_kernel(pt_ref, si_ref, *refs, new_blk):
    del pt_ref
    kv_refs = refs[:TOPK]
    q_ref, knew_ref, vnew_ref, ocmp_ref, owin_ref, gate_ref, y_ref = refs[TOPK:]
    b = pl.program_id(0)
    g = pl.program_id(1)
    qg = (q_ref[0, 0] * SCALE).astype(BF16)
    row = lax.broadcasted_iota(jnp.int32, (SEL_BLOCK, HEAD_DIM), 0)
    lane = lax.broadcasted_iota(jnp.int32, (1, TOPK * SEL_BLOCK), 1)
    kn = knew_ref[0, 0][0:1, :]
    vn = vnew_ref[0, 0][0:1, :]
    ks, vs = [], []
    masked = jnp.zeros((1, TOPK * SEL_BLOCK), jnp.bool_)
    lane_blk = lax.shift_right_logical(lane, 6)
    lane_off = lane & (SEL_BLOCK - 1)
    for kk in range(TOPK):
        is_new = si_ref[(b * N_KV + g) * TOPK + kk] == new_blk
        sub = row == jnp.where(is_new, 0, -1)
        k_blk = kv_refs[kk][pl.ds(2 * N_KV + g, SEL_BLOCK, stride=16), :]
        v_blk = kv_refs[kk][pl.ds(3 * N_KV + g, SEL_BLOCK, stride=16), :]
        ks.append(jnp.where(sub, kn, k_blk).astype(BF16))
        vs.append(jnp.where(sub, vn, v_blk).astype(BF16))
        masked = masked | ((lane_blk == jnp.where(is_new, kk, -1)) & (lane_off > 0))
    kall = jnp.concatenate(ks, axis=0)
    vall = jnp.concatenate(vs, axis=0)
    s = jnp.where(masked, NEG, _dot_nt(qg, kall))
    e = jnp.exp(s - jnp.max(s, axis=-1, keepdims=True))
    p = e / jnp.sum(e, axis=-1, keepdims=True)
    o_sel = _dot(p.astype(BF16), vall)
    gs = _sigmoid(gate_ref[0, 0])
    y_ref[0, 0] = gs[:, 0:1] * ocmp_ref[0, 0] + gs[:, 1:2] * o_sel + gs[:, 2:3] * owin_ref[0, 0]


def _dec2(pt, sidx, cache2, q4, knew4, vnew4, ocmp, owin, gate4, *, new_blk):
    nb = q4.shape[0]
    last = new_blk - 1

    def half_page(b, g, pt, si, kk):
        j = jnp.minimum(si[(b * N_KV + g) * TOPK + kk], last)
        return pt[b, lax.shift_right_logical(j, 1)] * 2 + (j & 1)

    kv_specs = [pl.BlockSpec((SEL_BLOCK * 16, 128), lambda b, g, pt, si, kk=kk: (half_page(b, g, pt, si, kk), 0))
                for kk in range(TOPK)]
    hg = pl.BlockSpec((1, 1, 8, HEAD_DIM), lambda b, g, pt, si: (b, g, 0, 0))
    return pl.pallas_call(
        functools.partial(_dec2_kernel, new_blk=new_blk),
        grid_spec=pltpu.PrefetchScalarGridSpec(
            num_scalar_prefetch=2,
            grid=(nb, N_KV),
            in_specs=kv_specs + [hg] * 6,
            out_specs=hg),
        out_shape=jax.ShapeDtypeStruct((nb, N_KV, 8, HEAD_DIM), F32),
        compiler_params=_cp(("parallel", "parallel")),
        name="dec_sel",
    )(pt, sidx, *([cache2] * TOPK), q4, knew4, vnew4, ocmp, owin, gate4)


def _pad_rows(a, rows):
    return jnp.pad(a, ((0, rows - a.shape[0]),) + ((0, 0),) * (a.ndim - 1))


def _to_hg(a2d, nb):
    g = a2d.shape[1] // (N_KV * HEAD_DIM)
    a = a2d.reshape(nb, N_KV, g, HEAD_DIM)
    return jnp.pad(a, ((0, 0), (0, 0), (0, 8 - g), (0, 0)))


def kernel(x_prompt, x_sample, cache_kv, cache_win, cache_mem, state_conv_mix, state_conv_ffn, page_table,
           mem_prompt, ln_g, ln_b, w_in_a, conv_a_w, w_in_b, w_o, w_mem_kv, w_up, ffn_conv_w, w_down,
           w_kv_shared, cmp_pos, w_cmp):
    nb, t = x_prompt.shape[:2]
    db = x_sample.shape[0]
    m_p = nb * t
    ms = 16
    tm = 512
    tpb = t // tm
    past = page_table.shape[1] * cache_kv.shape[1]

    w_in_a_b = w_in_a.astype(BF16)
    hq = N_HEADS * HEAD_DIM
    hgate = hq + 3 * N_HEADS
    wgate = w_in_b[:, :, hq:hgate].reshape(2, D_MODEL, 3, N_KV, GROUP)
    wgate = jnp.transpose(wgate, (0, 1, 3, 2, 4)).reshape(2, D_MODEL, N_KV, 3 * GROUP)
    wgate = jnp.pad(wgate, ((0, 0), (0, 0), (0, 0), (0, 128 - 3 * GROUP))).reshape(2, D_MODEL, N_KV * 128)
    w_qg_b = jnp.concatenate([w_in_b[:, :, :hq], wgate], axis=-1).astype(BF16)
    w_qm_b = jnp.concatenate([w_in_a[:, :, 3 * D_A:], w_in_b[:, :, hgate:]], axis=0).astype(BF16)
    w_o_b = w_o.astype(BF16)
    padf = D_FF_PAD - D_FF
    w_up_z = jnp.pad(w_up[:, :, :D_FF], ((0, 0), (0, 0), (0, padf))).astype(BF16)
    w_up_g = jnp.pad(w_up[:, :, D_FF:], ((0, 0), (0, 0), (0, padf))).astype(BF16)
    w_down_b = jnp.pad(w_down, ((0, 0), (0, padf), (0, 0))).astype(BF16)
    ffn_cw = jnp.pad(ffn_conv_w, ((0, 0), (0, 0), (0, padf)))
    w_kv_b = w_kv_shared.astype(BF16)
    w_mem_b = w_mem_kv.astype(BF16)
    w_cmp2 = jnp.concatenate([w_cmp[:, :CMP_STRIDE].reshape(2, 2048, 128),
                              w_cmp[:, CMP_STRIDE:].reshape(2, 2048, 128)], axis=-1).astype(BF16)
    pos8 = _posterm(jnp.broadcast_to(cmp_pos.reshape(2, 1, CMP_LEN * HEAD_DIM), (2, 8, CMP_LEN * HEAD_DIM)),
                    w_cmp.reshape(2, CMP_LEN * HEAD_DIM, HEAD_DIM))
    g2 = ln_g.reshape(2 * DEPTH, 1, D_MODEL)
    b2 = ln_b.reshape(2 * DEPTH, 1, D_MODEL)

    def overlap(n_cmp, n_lanes_j, n_rows):
        n = jnp.arange(n_rows)[:, None]
        j = jnp.arange(n_lanes_j)[None, :]
        return ((n >= 4 * j - 1) & (n <= 4 * j + 3) & (n < n_cmp)).astype(BF16)

    ovt_p = overlap(t // CMP_STRIDE - 1, 128, 128).T
    emat = (jnp.arange(t)[:, None] // SEL_BLOCK == jnp.arange(128)[None, :]).astype(BF16)
    n_half_s = past // CMP_STRIDE
    n_slc_s = past // SEL_BLOCK + 1
    ov_s = overlap(n_half_s - 1, 384, n_half_s)

    x = x_prompt.reshape(m_p, D_MODEL)
    mem_x = mem_prompt.reshape(nb * N_MEM, D_MODEL)
    mem_kv = [_mm(mem_x, w_mem_b, layer=l, tm=512, tn=2 * MEM_W, name="mm_memkv").reshape(nb, N_MEM, 2 * MEM_W)
              for l in range(DEPTH)]
    zeros_mix = jnp.zeros((nb, 8, D_A), F32)
    zeros_ffn = jnp.zeros((nb, 8, D_FF_PAD), F32)
    conv_p, ffn_p = [], []

    def tail(x, l, y, memkv_l, seq_prev8):
        mo = _memattn(x.reshape(nb, t, D_MODEL), w_qm_b, memkv_l, layer=l, tt=tm, out_dtype=BF16).reshape(m_p, MEM_W)
        x = _oproj(x, y, mo, w_o_b, g2, b2, layer=l, tm=tm)
        x, st = _ffn_seq(x, w_up_z, w_up_g, ffn_cw, w_down_b, seq_prev8, g2, b2,
                         layer=l, tm=tm, tf=512, tpb=tpb)
        return x, st

    for l in range(N_A):
        y, st = _amix_seq(x, w_in_a_b, conv_a_w, zeros_mix, layer=l, tm=tm, tn=512, tpb=tpb)
        conv_p.append(st[tpb - 1::tpb, 6:8])
        x, fs = tail(x, l, y, mem_kv[l], zeros_ffn)
        ffn_p.append(fs[tpb - 1::tpb, 6:8, :D_FF])

    kv_tr, kvb_rows = _kvrows(x, w_kv_b, tm=tm, tn=2048)
    kv_win, kvb_win = _mm(x, w_kv_b, tm=tm, tn=1024, n=1024, col0=2, out_dtypes=(F32, BF16), name="mm_kvwin")
    pt_p = jnp.arange(m_p // PAGE, dtype=jnp.int32).reshape(nb, t // PAGE)
    ab_p = _compress(kv_tr.reshape(m_p, 16, HEAD_DIM), pt_p, w_cmp2, npg=t // PAGE)

    for j in range(DEPTH - N_A):
        l = N_A + j
        pb = _mm(x, w_qg_b, layer=j, tm=tm, tn=2048, name="mm_qgate")
        y = _nsa_prompt(pb, kvb_rows, kvb_win, ab_p, pos8, ovt_p, emat, nb=nb, t=t, tq=256, ck=512)
        x, fs = tail(x, l, y, mem_kv[l], zeros_ffn)
        ffn_p.append(fs[tpb - 1::tpb, 6:8, :D_FF])
    y_prompt = x.reshape(nb, t, D_MODEL)

    xs = _pad_rows(x_sample.reshape(db, D_MODEL), ms)
    cmem_rows = cache_mem.reshape(DEPTH * db * N_MEM * 8, HEAD_DIM)
    prev_mix = jnp.pad(state_conv_mix.transpose(0, 2, 1, 3), ((0, 0), (0, 0), (0, ms - db), (0, 0)))
    prev_ffn = jnp.pad(state_conv_ffn.transpose(0, 2, 1, 3), ((0, 0), (0, 0), (0, ms - db), (0, padf)))
    cu_s, uz_s = [], []

    def tail_s(xs, l, y):
        xrep = jnp.broadcast_to(xs[:db, None, :], (db, 8, D_MODEL))
        mo = _pad_rows(_memattn_rows(xrep, w_qm_b, cmem_rows, layer=l, tt=8, out_dtype=F32)[:, 0], ms)
        xs = _oproj(xs, y, mo, w_o_b, g2, b2, layer=l, tm=ms)
        xs, uz = _ffn_tok(xs, w_up_z, w_up_g, ffn_cw, w_down_b, prev_ffn, g2, b2, layer=l, tf=512)
        uz_s.append(uz)
        return xs

    for l in range(N_A):
        y, cu = _amix_tok(xs, w_in_a_b, conv_a_w, prev_mix, layer=l, tn=512)
        cu_s.append(cu)
        xs = tail_s(xs, l, y)

    kv_s = _mm(xs, w_kv_b, tm=ms, tn=512, name="mm_kv_tok")[:db]
    n_tok = cache_kv.shape[0] * PAGE
    ab_s = _compress(cache_kv.reshape(n_tok, 16, HEAD_DIM), page_table, w_cmp2, npg=16)
    cache2 = cache_kv.reshape(n_tok * 16, HEAD_DIM)
    cwin = cache_win.reshape(db, WINDOW, 1024)
    ksel_new = _to_hg(kv_s[:, 1024:1536], db)
    vsel_new = _to_hg(kv_s[:, 1536:2048], db)
    kwin_new = _to_hg(kv_s[:, 2048:2560], db)
    vwin_new = _to_hg(kv_s[:, 2560:3072], db)

    for j in range(DEPTH - N_A):
        l = N_A + j
        pbs = _mm(xs, w_qg_b, layer=j, tm=ms, tn=512, name="mm_qgate_tok")[:db]
        q4 = _to_hg(pbs[:, :hq], db)
        gate4 = pbs[:, hq:].reshape(db, N_KV, 128)[:, :, :3 * GROUP].reshape(db, N_KV, 3, GROUP)
        gate4 = jnp.pad(gate4.transpose(0, 1, 3, 2), ((0, 0), (0, 0), (0, 8 - GROUP), (0, 128 - 3)))
        ocmp, owin, idx = _dec1(q4, ab_s, pos8, ov_s, cwin, kwin_new, vwin_new, n_slc=n_slc_s)
        sidx = idx[:, :N_KV, :TOPK].reshape(db * N_KV * TOPK)
        y4 = _dec2(page_table, sidx, cache2, q4, ksel_new, vsel_new, ocmp, owin, gate4, new_blk=n_slc_s - 1)
        y = _pad_rows(y4[:, :, :GROUP].reshape(db, D_A), ms).astype(BF16)
        xs = tail_s(xs, l, y)
    y_sample = xs[:db].reshape(db, 1, D_MODEL)

    wb = cache_win.shape[1]
    win_new = kv_s[:, 2048:].reshape(db, 1, 2, N_KV, HEAD_DIM)
    return (y_prompt, y_sample,
            kv_tr.reshape(nb, t, 4, N_KV, HEAD_DIM),
            kv_win.reshape(nb, t, 1024)[:, max(t - WINDOW, 0):].reshape(nb, min(t, WINDOW), 2, N_KV, HEAD_DIM),
            jnp.stack(mem_kv).reshape(DEPTH, nb, N_MEM, 2, 4, HEAD_DIM),
            jnp.stack(conv_p),
            jnp.stack(ffn_p),
            kv_s[:, :2048].reshape(db, 1, 4, N_KV, HEAD_DIM),
            jnp.concatenate([cache_win, win_new], axis=1)[:, -wb:],
            jnp.stack([state_conv_mix[:, :, 1], jnp.stack(cu_s)[:, :db]], axis=2),
            jnp.stack([state_conv_ffn[:, :, 1], jnp.stack(uz_s)[:, :db, :D_FF]], axis=2))
```

```python
import functools

import jax
import jax.numpy as jnp
from jax import lax
from jax.experimental import pallas as pl
from jax.experimental.pallas import tpu as pltpu

F32 = jnp.float32
BF16 = jnp.bfloat16

D_MODEL = 2048
DEPTH = 4
HEAD_DIM = 128
N_A = 2
D_A = 1536
N_HEADS = 12
N_KV = 4
GROUP = 3
MEM_W = 512
N_MEM = 256
D_FF = 5504
D_FF_PAD = 5632
CMP_STRIDE = 16
CMP_LEN = 32
SEL_BLOCK = 64
TOPK = 16
WINDOW = 512
PAGE = 128
ALPHA = (2 * DEPTH) ** 0.25
LN_EPS = 1e-5
NEG = -1e30
FORCE = 1e9
SCALE = HEAD_DIM ** -0.5
LOG2E = 1.4426950408889634

VMEM_LIMIT = 56 * 1024 * 1024


def _cp(sem):
    return pltpu.CompilerParams(dimension_semantics=sem, vmem_limit_bytes=VMEM_LIMIT)


def _dot(a, b):
    return jnp.dot(a, b, preferred_element_type=F32)


def _wspec(block, index_map, layer=None):
    if layer is None:
        return pl.BlockSpec(block, index_map)
    return pl.BlockSpec((None,) + block, lambda *a: (layer,) + index_map(*a))


def _dot_nt(a, b):
    return lax.dot_general(a, b, (((1,), (1,)), ((), ())), preferred_element_type=F32)


def _layer_norm(v, g, b):
    mu = jnp.mean(v, axis=-1, keepdims=True)
    d = v - mu
    var = jnp.mean(d * d, axis=-1, keepdims=True)
    return d * lax.rsqrt(var + LN_EPS) * g + b


def _sigmoid(x):
    return 1.0 / (1.0 + jnp.exp(-x))


def _fold(x, op):
    r = x[:, 0:128]
    for i in range(1, x.shape[1] // 128):
        r = op(r, x[:, i * 128:(i + 1) * 128])
    return r


def _mm_kernel(x_ref, w_ref, *rest, n_out):
    o_refs = rest[:n_out]
    xb_ref = rest[n_out]

    @pl.when(pl.program_id(1) == 0)
    def _():
        xb_ref[...] = x_ref[...].astype(BF16)

    r = _dot(xb_ref[...], w_ref[...])
    for o in o_refs:
        o[...] = r.astype(o.dtype)


def _mm(x, w, *, tm, tn, n=None, col0=0, layer=None, out_dtypes=(F32,), name="mm"):
    m, k = x.shape
    n = w.shape[-1] if n is None else n
    outs = pl.pallas_call(
        functools.partial(_mm_kernel, n_out=len(out_dtypes)),
        grid=(m // tm, n // tn),
        in_specs=[pl.BlockSpec((tm, k), lambda i, j: (i, 0)),
                  _wspec((k, tn), lambda i, j: (0, col0 + j), layer)],
        out_specs=[pl.BlockSpec((tm, tn), lambda i, j: (i, j)) for _ in out_dtypes],
        out_shape=[jax.ShapeDtypeStruct((m, n), d) for d in out_dtypes],
        scratch_shapes=[pltpu.VMEM((tm, k), BF16)],
        compiler_params=_cp(("parallel", "arbitrary")),
        name=name,
    )(x, w)
    return outs if len(out_dtypes) > 1 else outs[0]


def _kvrows_kernel(x_ref, w_ref, tr_ref, ob_ref, xb_ref):
    j = pl.program_id(1)

    @pl.when(j == 0)
    def _():
        xb_ref[...] = x_ref[...].astype(BF16)

    r = _dot(xb_ref[...], w_ref[...])
    ob_ref[...] = r.astype(BF16)
    tm = r.shape[0]
    for c in range(r.shape[1] // HEAD_DIM):
        tr_ref[pl.ds(j * (r.shape[1] // HEAD_DIM) + c, tm, stride=16), :] = r[:, c * HEAD_DIM:(c + 1) * HEAD_DIM]


def _kvrows(x, w, *, tm, tn):
    m, k = x.shape
    n = 16 * HEAD_DIM
    return pl.pallas_call(
        _kvrows_kernel,
        grid=(m // tm, n // tn),
        in_specs=[pl.BlockSpec((tm, k), lambda i, j: (i, 0)),
                  pl.BlockSpec((k, tn), lambda i, j: (0, j))],
        out_specs=[pl.BlockSpec((tm * 16, HEAD_DIM), lambda i, j: (i, 0)),
                   pl.BlockSpec((tm, tn), lambda i, j: (i, j))],
        out_shape=[jax.ShapeDtypeStruct((m * 16, HEAD_DIM), F32),
                   jax.ShapeDtypeStruct((m, n), BF16)],
        scratch_shapes=[pltpu.VMEM((tm, k), BF16)],
        compiler_params=_cp(("parallel", "arbitrary")),
        name="kvrows",
    )(x, w)


def _conv_rows(v, cw_ref, tail8):
    row = lax.broadcasted_iota(jnp.int32, v.shape, 0)
    t0 = tail8[6:7, :]
    t1 = tail8[7:8, :]
    s1 = jnp.where(row == 0, t1, pltpu.roll(v, 1, 0))
    s2 = jnp.where(row == 0, t0, jnp.where(row == 1, t1, pltpu.roll(v, 2, 0)))
    return cw_ref[0:1, :] * s2 + cw_ref[1:2, :] * s1 + cw_ref[2:3, :] * v


def _amix_seq_kernel(x_ref, wu_ref, wb_ref, wc_ref, cw_ref, prev_ref, y_ref, st_ref,
                     xb_ref, carry_ref, *, tpb):
    i = pl.program_id(0)
    j = pl.program_id(1)

    @pl.when(j == 0)
    def _():
        xb_ref[...] = x_ref[...].astype(BF16)

    xb = xb_ref[...]
    cu = _dot(xb, wc_ref[...]) * _dot(xb, wu_ref[...])
    tail8 = jnp.where((i % tpb) == 0, prev_ref[0], carry_ref[j])
    conv = _conv_rows(cu, cw_ref, tail8)
    y_ref[...] = (_dot(xb, wb_ref[...]) * conv).astype(y_ref.dtype)
    last8 = cu[cu.shape[0] - 8:, :]
    carry_ref[j] = last8
    st_ref[0] = last8


def _amix_seq(x, w_in, cw, prev8, *, layer, tm, tn, tpb):
    m = x.shape[0]
    nj = D_A // tn
    return pl.pallas_call(
        functools.partial(_amix_seq_kernel, tpb=tpb),
        grid=(m // tm, nj),
        in_specs=[pl.BlockSpec((tm, D_MODEL), lambda i, j: (i, 0)),
                  _wspec((D_MODEL, tn), lambda i, j: (0, j), layer),
                  _wspec((D_MODEL, tn), lambda i, j: (0, nj + j), layer),
                  _wspec((D_MODEL, tn), lambda i, j: (0, 2 * nj + j), layer),
                  _wspec((3, tn), lambda i, j: (0, j), layer),
                  pl.BlockSpec((1, 8, tn), lambda i, j: (i // tpb, 0, j))],
        out_specs=[pl.BlockSpec((tm, tn), lambda i, j: (i, j)),
                   pl.BlockSpec((1, 8, tn), lambda i, j: (i, 0, j))],
        out_shape=[jax.ShapeDtypeStruct((m, D_A), BF16),
                   jax.ShapeDtypeStruct((m // tm, 8, D_A), F32)],
        scratch_shapes=[pltpu.VMEM((tm, D_MODEL), BF16),
                        pltpu.VMEM((nj, 8, tn), F32)],
        compiler_params=_cp(("arbitrary", "arbitrary")),
        name="amix_seq",
    )(x, w_in, w_in, w_in, cw, prev8)


def _amix_tok_kernel(x_ref, wu_ref, wb_ref, wc_ref, cw_ref, p0_ref, p1_ref, y_ref, cu_ref):
    xb = x_ref[...].astype(BF16)
    cu = _dot(xb, wc_ref[...]) * _dot(xb, wu_ref[...])
    conv = cw_ref[0:1, :] * p0_ref[...] + cw_ref[1:2, :] * p1_ref[...] + cw_ref[2:3, :] * cu
    y_ref[...] = (_dot(xb, wb_ref[...]) * conv).astype(y_ref.dtype)
    cu_ref[...] = cu


def _amix_tok(x, w_in, cw, prev, *, layer, tn):
    m = x.shape[0]
    nj = D_A // tn
    return pl.pallas_call(
        _amix_tok_kernel,
        grid=(nj,),
        in_specs=[pl.BlockSpec((m, D_MODEL), lambda j: (0, 0)),
                  _wspec((D_MODEL, tn), lambda j: (0, j), layer),
                  _wspec((D_MODEL, tn), lambda j: (0, nj + j), layer),
                  _wspec((D_MODEL, tn), lambda j: (0, 2 * nj + j), layer),
                  _wspec((3, tn), lambda j: (0, j), layer),
                  pl.BlockSpec((None, None, m, tn), lambda j: (layer, 0, 0, j)),
                  pl.BlockSpec((None, None, m, tn), lambda j: (layer, 1, 0, j))],
        out_specs=[pl.BlockSpec((m, tn), lambda j: (0, j)),
                   pl.BlockSpec((m, tn), lambda j: (0, j))],
        out_shape=[jax.ShapeDtypeStruct((m, D_A), BF16),
                   jax.ShapeDtypeStruct((m, D_A), F32)],
        compiler_params=_cp(("parallel",)),
        name="amix_tok",
    )(x, w_in, w_in, w_in, cw, prev, prev)


def _memattn_body(x_ref, wq_ref, o_ref, key, val):
    xb = x_ref[0].astype(BF16)
    qm = _dot(xb, wq_ref[...]) * SCALE
    for h in range(4):
        sl = slice(h * HEAD_DIM, (h + 1) * HEAD_DIM)
        q = qm[:, sl].astype(BF16)
        s = _dot_nt(q, key(h).astype(BF16))
        e = jnp.exp(s - jnp.max(s, axis=-1, keepdims=True))
        p = e / jnp.sum(e, axis=-1, keepdims=True)
        o_ref[0, :, sl] = _dot(p.astype(BF16), val(h).astype(BF16)).astype(o_ref.dtype)


def _memattn_rows_kernel(x_ref, wq_ref, mkv_ref, o_ref):
    _memattn_body(x_ref, wq_ref, o_ref,
                  lambda h: mkv_ref[pl.ds(h, N_MEM, stride=8), :],
                  lambda h: mkv_ref[pl.ds(4 + h, N_MEM, stride=8), :])


def _memattn_rows(x3, wq, memrows, *, layer, tt, out_dtype):
    nb, t, _ = x3.shape
    return pl.pallas_call(
        _memattn_rows_kernel,
        grid=(nb, t // tt),
        in_specs=[pl.BlockSpec((1, tt, D_MODEL), lambda b, i: (b, i, 0)),
                  _wspec((D_MODEL, MEM_W), lambda b, i: (0, 0), layer),
                  pl.BlockSpec((N_MEM * 8, HEAD_DIM), lambda b, i: (layer * nb + b, 0))],
        out_specs=pl.BlockSpec((1, tt, MEM_W), lambda b, i: (b, i, 0)),
        out_shape=jax.ShapeDtypeStruct((nb, t, MEM_W), out_dtype),
        compiler_params=_cp(("parallel", "parallel")),
        name="memattn_rows",
    )(x3, wq, memrows)


def _oproj_kernel(x_ref, y_ref, m_ref, wy_ref, wm_ref, g_ref, b_ref, o_ref):
    acc = _dot(y_ref[...].astype(BF16), wy_ref[...]) + _dot(m_ref[...].astype(BF16), wm_ref[...])
    o_ref[...] = _layer_norm(ALPHA * x_ref[...] + acc, g_ref[...], b_ref[...])


def _oproj(x, y, mo, wo, g, b, *, layer, tm):
    gb = 2 * layer
    m = x.shape[0]
    return pl.pallas_call(
        _oproj_kernel,
        grid=(m // tm,),
        in_specs=[pl.BlockSpec((tm, D_MODEL), lambda i: (i, 0)),
                  pl.BlockSpec((tm, D_A), lambda i: (i, 0)),
                  pl.BlockSpec((tm, MEM_W), lambda i: (i, 0)),
                  _wspec((D_A, D_MODEL), lambda i: (0, 0), layer),
                  _wspec((MEM_W, D_MODEL), lambda i: (D_A // MEM_W, 0), layer),
                  _wspec((1, D_MODEL), lambda i: (0, 0), gb),
                  _wspec((1, D_MODEL), lambda i: (0, 0), gb)],
        out_specs=pl.BlockSpec((tm, D_MODEL), lambda i: (i, 0)),
        out_shape=jax.ShapeDtypeStruct((m, D_MODEL), F32),
        compiler_params=_cp(("parallel",)),
        name="oproj_ln",
    )(x, y, mo, wo, wo, g, b)


def _oproj_mem_kernel(x_ref, y_ref, wq_ref, mk_ref, mv_ref, wy_ref, wm_ref, g_ref, b_ref, o_ref):
    x = x_ref[...]
    qm = _dot(x.astype(BF16), wq_ref[...]) * SCALE
    acc = _dot(y_ref[...].astype(BF16), wy_ref[...])
    mo = []
    for h in range(4):
        sl = slice(h * HEAD_DIM, (h + 1) * HEAD_DIM)
        s = _dot_nt(qm[:, sl].astype(BF16), mk_ref[0, :, sl].astype(BF16))
        e = jnp.exp(s - jnp.max(s, axis=-1, keepdims=True))
        p = e / jnp.sum(e, axis=-1, keepdims=True)
        mo.append(_dot(p.astype(BF16), mv_ref[0, :, sl].astype(BF16)).astype(BF16))
    acc = acc + _dot(jnp.concatenate(mo, axis=1), wm_ref[...])
    o_ref[...] = _layer_norm(ALPHA * x + acc, g_ref[...], b_ref[...])


def _oproj_mem(x, y, wq, memkv, wo, g, b, *, layer, tm, tpb):
    gb = 2 * layer
    m = x.shape[0]
    return pl.pallas_call(
        _oproj_mem_kernel,
        grid=(m // tm,),
        in_specs=[pl.BlockSpec((tm, D_MODEL), lambda i: (i, 0)),
                  pl.BlockSpec((tm, D_A), lambda i: (i, 0)),
                  _wspec((D_MODEL, MEM_W), lambda i: (0, 0), layer),
                  pl.BlockSpec((1, N_MEM, MEM_W), lambda i: (i // tpb, 0, 0)),
                  pl.BlockSpec((1, N_MEM, MEM_W), lambda i: (i // tpb, 0, 1)),
                  _wspec((D_A, D_MODEL), lambda i: (0, 0), layer),
                  _wspec((MEM_W, D_MODEL), lambda i: (D_A // MEM_W, 0), layer),
                  _wspec((1, D_MODEL), lambda i: (0, 0), gb),
                  _wspec((1, D_MODEL), lambda i: (0, 0), gb)],
        out_specs=pl.BlockSpec((tm, D_MODEL), lambda i: (i, 0)),
        out_shape=jax.ShapeDtypeStruct((m, D_MODEL), F32),
        compiler_params=_cp(("parallel",)),
        name="oproj_mem_ln",
    )(x, y, wq, memkv, memkv, wo, wo, g, b)


def _ffn_seq_kernel(x_ref, wz_ref, wg_ref, cw_ref, wd_ref, prev_ref, g_ref, b_ref,
                    o_ref, st_ref, xb_ref, acc_ref, carry_ref, *, tpb):
    i = pl.program_id(0)
    j = pl.program_id(1)

    @pl.when(j == 0)
    def _():
        xb_ref[...] = x_ref[...].astype(BF16)
        acc_ref[...] = jnp.zeros_like(acc_ref)

    xb = xb_ref[...]
    uz = _dot(xb, wz_ref[...])
    tail8 = jnp.where((i % tpb) == 0, prev_ref[0], carry_ref[j])
    z = _conv_rows(uz, cw_ref, tail8)
    h = z * _sigmoid(z) * _dot(xb, wg_ref[...])
    acc_ref[...] += _dot(h.astype(BF16), wd_ref[...])
    last8 = uz[uz.shape[0] - 8:, :]
    carry_ref[j] = last8
    st_ref[0] = last8

    @pl.when(j == pl.num_programs(1) - 1)
    def _():
        o_ref[...] = _layer_norm(ALPHA * x_ref[...] + acc_ref[...], g_ref[...], b_ref[...])


def _ffn_seq(x, wz, wg, cw, wd, prev8, g, b, *, layer, tm, tf, tpb):
    gb = 2 * layer + 1
    m = x.shape[0]
    nj = D_FF_PAD // tf
    return pl.pallas_call(
        functools.partial(_ffn_seq_kernel, tpb=tpb),
        grid=(m // tm, nj),
        in_specs=[pl.BlockSpec((tm, D_MODEL), lambda i, j: (i, 0)),
                  _wspec((D_MODEL, tf), lambda i, j: (0, j), layer),
                  _wspec((D_MODEL, tf), lambda i, j: (0, j), layer),
                  _wspec((3, tf), lambda i, j: (0, j), layer),
                  _wspec((tf, D_MODEL), lambda i, j: (j, 0), layer),
                  pl.BlockSpec((1, 8, tf), lambda i, j: (i // tpb, 0, j)),
                  _wspec((1, D_MODEL), lambda i, j: (0, 0), gb),
                  _wspec((1, D_MODEL), lambda i, j: (0, 0), gb)],
        out_specs=[pl.BlockSpec((tm, D_MODEL), lambda i, j: (i, 0)),
                   pl.BlockSpec((1, 8, tf), lambda i, j: (i, 0, j))],
        out_shape=[jax.ShapeDtypeStruct((m, D_MODEL), F32),
                   jax.ShapeDtypeStruct((m // tm, 8, D_FF_PAD), F32)],
        scratch_shapes=[pltpu.VMEM((tm, D_MODEL), BF16),
                        pltpu.VMEM((tm, D_MODEL), F32),
                        pltpu.VMEM((nj, 8, tf), F32)],
        compiler_params=_cp(("arbitrary", "arbitrary")),
        name="ffn_seq",
    )(x, wz, wg, cw, wd, prev8, g, b)


def _ffn_tok_kernel(x_ref, wz_ref, wg_ref, cw_ref, wd_ref, p0_ref, p1_ref, g_ref, b_ref,
                    o_ref, uz_ref, acc_ref):
    j = pl.program_id(0)

    @pl.when(j == 0)
    def _():
        acc_ref[...] = jnp.zeros_like(acc_ref)

    xb = x_ref[...].astype(BF16)
    uz = _dot(xb, wz_ref[...])
    z = cw_ref[0:1, :] * p0_ref[...] + cw_ref[1:2, :] * p1_ref[...] + cw_ref[2:3, :] * uz
    h = z * _sigmoid(z) * _dot(xb, wg_ref[...])
    acc_ref[...] += _dot(h.astype(BF16), wd_ref[...])
    uz_ref[...] = uz

    @pl.when(j == pl.num_programs(0) - 1)
    def _():
        o_ref[...] = _layer_norm(ALPHA * x_ref[...] + acc_ref[...], g_ref[...], b_ref[...])


def _ffn_tok(x, wz, wg, cw, wd, prev, g, b, *, layer, tf):
    gb = 2 * layer + 1
    m = x.shape[0]
    nj = D_FF_PAD // tf
    return pl.pallas_call(
        _ffn_tok_kernel,
        grid=(nj,),
        in_specs=[pl.BlockSpec((m, D_MODEL), lambda j: (0, 0)),
                  _wspec((D_MODEL, tf), lambda j: (0, j), layer),
                  _wspec((D_MODEL, tf), lambda j: (0, j), layer),
                  _wspec((3, tf), lambda j: (0, j), layer),
                  _wspec((tf, D_MODEL), lambda j: (j, 0), layer),
                  pl.BlockSpec((None, None, m, tf), lambda j: (layer, 0, 0, j)),
                  pl.BlockSpec((None, None, m, tf), lambda j: (layer, 1, 0, j)),
                  _wspec((1, D_MODEL), lambda j: (0, 0), gb),
                  _wspec((1, D_MODEL), lambda j: (0, 0), gb)],
        out_specs=[pl.BlockSpec((m, D_MODEL), lambda j: (0, 0)),
                   pl.BlockSpec((m, tf), lambda j: (0, j))],
        out_shape=[jax.ShapeDtypeStruct((m, D_MODEL), F32),
                   jax.ShapeDtypeStruct((m, D_FF_PAD), F32)],
        scratch_shapes=[pltpu.VMEM((m, D_MODEL), F32)],
        compiler_params=_cp(("arbitrary",)),
        name="ffn_tok",
    )(x, wz, wg, cw, wd, prev, prev, g, b)


def _compress_kernel(pt_ref, *refs, npg):
    del pt_ref
    page_refs = refs[:npg]
    w_ref, o_ref, x2_ref = refs[npg:]
    nh = npg * 8
    for pg in range(npg):
        for r in range(PAGE):
            n = pg * 8 + r // CMP_STRIDE
            x2_ref[r % CMP_STRIDE, n * 8:n * 8 + 8, :] = page_refs[pg][r]
    for c in range(2):
        xs = [jnp.concatenate([x2_ref[s, pl.ds(c * N_KV + g, nh, stride=8), :] for s in range(CMP_STRIDE)], axis=1)
              for g in range(N_KV)]
        r = _dot(jnp.concatenate(xs, axis=0).astype(BF16), w_ref[c])
        for g in range(N_KV):
            o_ref[0, :, c * 1024 + g * 128:c * 1024 + (g + 1) * 128] = r[g * nh:(g + 1) * nh, 0:128]
            o_ref[0, :, c * 1024 + 512 + g * 128:c * 1024 + 512 + (g + 1) * 128] = r[g * nh:(g + 1) * nh, 128:256]


def _compress(rows3, pt, w2, *, npg):
    nb, npages = pt.shape
    nh = npg * 8
    page_specs = [pl.BlockSpec((PAGE, 8, 128), lambda b, s, pt, k=k: (pt[b, s * npg + k], 0, 0))
                  for k in range(npg)]
    return pl.pallas_call(
        functools.partial(_compress_kernel, npg=npg),
        grid_spec=pltpu.PrefetchScalarGridSpec(
            num_scalar_prefetch=1,
            grid=(nb, npages // npg),
            in_specs=page_specs + [pl.BlockSpec((2, 2048, 256), lambda b, s, pt: (0, 0, 0))],
            out_specs=pl.BlockSpec((1, nh, 2048), lambda b, s, pt: (b, s, 0)),
            scratch_shapes=[pltpu.VMEM((CMP_STRIDE, nh * 8, 128), F32)]),
        out_shape=jax.ShapeDtypeStruct((nb, npages * 8, 2048), F32),
        compiler_params=_cp(("parallel", "parallel")),
        name="compress",
    )(pt, *([rows3] * npg), w2)


def _posterm_kernel(pos_ref, w_ref, o_ref):
    o_ref[...] = jnp.zeros_like(o_ref)
    for c in range(2):
        x = pos_ref[c]
        w = w_ref[c]
        xh = x.astype(BF16)
        xl = (x - xh.astype(F32)).astype(BF16)
        wh = w.astype(BF16)
        wl = (w - wh.astype(F32)).astype(BF16)
        r = _dot(xh, wh) + _dot(xh, wl) + _dot(xl, wh)
        o_ref[c:c + 1, :] = r[0:1, :]


def _posterm(pos8, w2):
    return pl.pallas_call(
        _posterm_kernel,
        out_shape=jax.ShapeDtypeStruct((8, HEAD_DIM), F32),
        compiler_params=pltpu.CompilerParams(vmem_limit_bytes=VMEM_LIMIT),
        name="posterm",
    )(pos8, w2)


def _nsa_kernel(q_ref, gate_ref, ksel_ref, vsel_ref, kwin_ref, vwin_ref,
                ak_ref, bk_ref, av_ref, bv_ref, pos_ref, ovt_ref, e_ref,
                y_ref, kcat_ref, cmpk_ref, cmpv_ref, bias_ref, *, tq, ck):
    qi = pl.program_id(2)
    t0 = pl.multiple_of(qi * tq, tq)
    n_half = ak_ref.shape[1]
    heads = lambda a: [a[r * tq:(r + 1) * tq] for r in range(GROUP)]

    @pl.when(qi == 0)
    def _():
        kcat_ref[:, 0:HEAD_DIM] = ksel_ref[...]
        kcat_ref[:, HEAD_DIM:2 * HEAD_DIM] = e_ref[...]
        cmpk_ref[...] = (ak_ref[0] + pltpu.roll(bk_ref[0], n_half - 1, 0) + pos_ref[0:1, :]).astype(BF16)
        cmpv_ref[...] = (av_ref[0] + pltpu.roll(bv_ref[0], n_half - 1, 0) + pos_ref[1:2, :]).astype(BF16)

    q = q_ref[...] * (SCALE * LOG2E)
    qs = jnp.concatenate([q[:, r * HEAD_DIM:(r + 1) * HEAD_DIM] for r in range(GROUP)], axis=0).astype(BF16)
    tcol = t0 + lax.broadcasted_iota(jnp.int32, (tq, 1), 0)

    n_l = lax.broadcasted_iota(jnp.int32, (1, n_half), 1)
    cvalid = (n_l * CMP_STRIDE + (CMP_LEN - 1) <= tcol) & (n_l < n_half - 1)
    cbias = jnp.where(cvalid, 0.0, NEG)
    ckeep = jnp.where(cvalid, 1.0, 0.0)
    s = _dot_nt(qs, cmpk_ref[...])
    s = jnp.concatenate([sh + cbias for sh in heads(s)], axis=0)
    e = jnp.exp2(s - jnp.max(s, axis=-1, keepdims=True))
    e = jnp.concatenate([eh * ckeep for eh in heads(e)], axis=0)
    l = jnp.sum(e, axis=-1, keepdims=True)
    p = e / jnp.where(l > 0.0, l, 1.0)
    o_cmp = _dot(p.astype(BF16), cmpv_ref[...])

    sub = 128
    wlen = WINDOW + sub
    win_parts = []
    for r0 in range(0, tq, sub):
        ws = pl.multiple_of(jnp.maximum(t0 + r0 - WINDOW, 0), sub)
        kpos = ws + lax.broadcasted_iota(jnp.int32, (1, wlen), 1)
        tsub = tcol[r0:r0 + sub]
        wbias = jnp.where((kpos <= tsub) & (kpos > tsub - WINDOW), 0.0, NEG)
        qsub = jnp.concatenate([qh[r0:r0 + sub] for qh in heads(qs)], axis=0)
        sw = _dot_nt(qsub, kwin_ref[pl.ds(ws, wlen), :])
        sw = jnp.concatenate([sw[r * sub:(r + 1) * sub] + wbias for r in range(GROUP)], axis=0)
        ew = jnp.exp2(sw - jnp.max(_fold(sw, jnp.maximum), axis=-1, keepdims=True))
        pw = ew / jnp.sum(_fold(ew, jnp.add), axis=-1, keepdims=True)
        win_parts.append(_dot(pw.astype(BF16), vwin_ref[pl.ds(ws, wlen), :]))
    o_win = jnp.concatenate([part[r * sub:(r + 1) * sub] for r in range(GROUP) for part in win_parts], axis=0)

    jj = lax.broadcasted_iota(jnp.int32, (32, tq), 0)
    tl = t0 + lax.broadcasted_iota(jnp.int32, (32, tq), 1)
    elig = jj * SEL_BLOCK <= tl

    def store_bias(sel_mask):
        bias_t = jnp.concatenate([jnp.where(sel_mask, 0.0, -FORCE), jnp.zeros((96, tq), F32)], axis=0)
        bias_ref[...] = bias_t.T.astype(BF16)

    @pl.when(t0 + tq <= TOPK * SEL_BLOCK)
    def _():
        store_bias(elig)

    @pl.when(t0 + tq > TOPK * SEL_BLOCK)
    def _():
        ps = p[0:tq] + p[tq:2 * tq] + p[2 * tq:3 * tq]
        ph = ps.astype(BF16)
        plo = (ps - ph.astype(F32)).astype(BF16)
        imp = (_dot_nt(ovt_ref[...], ph) + _dot_nt(ovt_ref[...], plo))[0:32, :]
        cur = lax.shift_right_logical(tl, 6)
        forced = (jj == 0) | (jj == cur) | (jj == cur - 1)
        score = jnp.where(forced, FORCE, jnp.where(elig, imp, -FORCE))
        sel = jnp.zeros((32, tq), F32)
        for k in range(32):
            row = score[k:k + 1, :]
            beats = (score > row) | ((score == row) & (jj < k))
            cnt = jnp.sum(beats.astype(F32), axis=0, keepdims=True)
            sel = jnp.where(jj == k, (cnt < float(TOPK)).astype(F32), sel)
        store_bias(sel > 0.0)

    qcat = jnp.concatenate([qs, jnp.concatenate([bias_ref[...]] * GROUP, axis=0)], axis=1)
    tcol3 = jnp.concatenate([tcol] * GROUP, axis=0)

    def body(c, carry):
        m, l, acc = carry
        k0 = pl.multiple_of(c * ck, ck)
        sc = _dot_nt(qcat, kcat_ref[pl.ds(k0, ck), :])
        kpos = k0 + lax.broadcasted_iota(jnp.int32, (1, ck), 1)
        sc = jnp.where(kpos <= tcol3, sc, NEG)
        mn = jnp.maximum(m, jnp.max(sc, axis=-1, keepdims=True))
        a = jnp.exp2(m - mn)
        pr = jnp.exp2(sc - mn)
        l = a * l + jnp.sum(pr, axis=-1, keepdims=True)
        acc = a * acc + _dot(pr.astype(BF16), vsel_ref[pl.ds(k0, ck), :])
        return mn, l, acc

    nch = (t0 + tq + ck - 1) // ck
    m0 = jnp.full((GROUP * tq, 1), NEG, F32)
    l0 = jnp.zeros((GROUP * tq, 1), F32)
    a0 = jnp.zeros((GROUP * tq, HEAD_DIM), F32)
    _, l, acc = lax.fori_loop(0, nch, body, (m0, l0, a0))
    o_sel = acc / l

    gs = _sigmoid(gate_ref[...])
    for r in range(GROUP):
        rs = slice(r * tq, (r + 1) * tq)
        o = (gs[:, r:r + 1] * o_cmp[rs] + gs[:, GROUP + r:GROUP + r + 1] * o_sel[rs]
             + gs[:, 2 * GROUP + r:2 * GROUP + r + 1] * o_win[rs])
        y_ref[:, r * HEAD_DIM:(r + 1) * HEAD_DIM] = o.astype(y_ref.dtype)


def _nsa_prompt(pb, kvb_rows, kvb_win, ab, pos8, ovt, emat, *, nb, t, tq, ck):
    nq = t // tq
    gw = GROUP * HEAD_DIM
    n_half = ab.shape[1]
    return pl.pallas_call(
        functools.partial(_nsa_kernel, tq=tq, ck=ck),
        grid=(nb, N_KV, nq),
        in_specs=[pl.BlockSpec((tq, gw), lambda b, g, i: (b * nq + i, g)),
                  pl.BlockSpec((tq, 128), lambda b, g, i: (b * nq + i, D_A // 128 + g)),
                  pl.BlockSpec((t, 128), lambda b, g, i: (b, 8 + g)),
                  pl.BlockSpec((t, 128), lambda b, g, i: (b, 12 + g)),
                  pl.BlockSpec((t, 128), lambda b, g, i: (b, g)),
                  pl.BlockSpec((t, 128), lambda b, g, i: (b, 4 + g)),
                  pl.BlockSpec((1, n_half, 128), lambda b, g, i: (b, 0, g)),
                  pl.BlockSpec((1, n_half, 128), lambda b, g, i: (b, 0, 4 + g)),
                  pl.BlockSpec((1, n_half, 128), lambda b, g, i: (b, 0, 8 + g)),
                  pl.BlockSpec((1, n_half, 128), lambda b, g, i: (b, 0, 12 + g)),
                  pl.BlockSpec((8, 128), lambda b, g, i: (0, 0)),
                  pl.BlockSpec((128, 128), lambda b, g, i: (0, 0)),
                  pl.BlockSpec((t, 128), lambda b, g, i: (0, 0))],
        out_specs=pl.BlockSpec((tq, gw), lambda b, g, i: (b * nq + i, g)),
        out_shape=jax.ShapeDtypeStruct((nb * t, D_A), BF16),
        scratch_shapes=[pltpu.VMEM((t, 2 * HEAD_DIM), BF16),
                        pltpu.VMEM((n_half, HEAD_DIM), BF16),
                        pltpu.VMEM((n_half, HEAD_DIM), BF16),
                        pltpu.VMEM((tq, 128), BF16)],
        compiler_params=_cp(("parallel", "parallel", "arbitrary")),
        name="nsa_prompt",
    )(pb, pb, kvb_rows, kvb_rows, kvb_win, kvb_win, ab, ab, ab, ab, pos8, ovt, emat)


def _dec1_kernel(q_ref, ab_ref, pos_ref, ov_ref, cwin_ref, knew_ref, vnew_ref,
                 ocmp_ref, owin_ref, idx_ref, *, n_slc):
    n_half = ab_ref.shape[1]
    nlane = ov_ref.shape[1]
    lane = lax.broadcasted_iota(jnp.int32, (8, nlane), 1)
    row8 = lax.broadcasted_iota(jnp.int32, (8, nlane), 0)
    imp_all = jnp.zeros((8, nlane), F32)
    for g in range(N_KV):
        sl = slice(g * HEAD_DIM, (g + 1) * HEAD_DIM)
        qg = (q_ref[0, g] * SCALE).astype(BF16)
        cmpk = (ab_ref[0, :, sl] + pltpu.roll(ab_ref[0, :, 512 + g * 128:512 + (g + 1) * 128], n_half - 1, 0)
                + pos_ref[0:1, :]).astype(BF16)
        cmpv = (ab_ref[0, :, 1024 + g * 128:1024 + (g + 1) * 128]
                + pltpu.roll(ab_ref[0, :, 1536 + g * 128:1536 + (g + 1) * 128], n_half - 1, 0)
                + pos_ref[1:2, :]).astype(BF16)
        n_l = lax.broadcasted_iota(jnp.int32, (1, n_half), 1)
        cm = n_l < n_half - 1
        s = jnp.where(cm, _dot_nt(qg, cmpk), NEG)
        e = jnp.where(cm, jnp.exp(s - jnp.max(s, axis=-1, keepdims=True)), 0.0)
        p = e / jnp.sum(e, axis=-1, keepdims=True)
        ocmp_ref[0, g] = _dot(p.astype(BF16), cmpv)
        ps = p[0:1] + p[1:2] + p[2:3]
        ps8 = jnp.broadcast_to(ps, (8, n_half))
        ph = ps8.astype(BF16)
        plo = (ps8 - ph.astype(F32)).astype(BF16)
        imp = _dot(ph, ov_ref[...]) + _dot(plo, ov_ref[...])
        imp_all = jnp.where(row8 == g, imp, imp_all)

        kc = cwin_ref[0, :, sl].astype(BF16)
        vc = cwin_ref[0, :, 512 + g * 128:512 + (g + 1) * 128].astype(BF16)
        kn = knew_ref[0, g][0:1, :].astype(BF16).astype(F32)
        vn = vnew_ref[0, g][0:1, :].astype(BF16).astype(F32)
        w_l = lax.broadcasted_iota(jnp.int32, (1, kc.shape[0]), 1)
        sc = jnp.where(w_l >= 1, _dot_nt(qg, kc), NEG)
        sn = jnp.sum(qg.astype(F32) * kn, axis=-1, keepdims=True)
        m = jnp.maximum(jnp.max(sc, axis=-1, keepdims=True), sn)
        ec = jnp.exp(sc - m)
        en = jnp.exp(sn - m)
        lw = jnp.sum(ec, axis=-1, keepdims=True) + en
        owin_ref[0, g] = _dot((ec / lw).astype(BF16), vc) + (en / lw).astype(BF16).astype(F32) * vn

    cur = n_slc - 1
    forced = (lane == 0) | (lane == cur) | (lane == cur - 1)
    score = jnp.where(forced, FORCE, jnp.where(lane < n_slc, imp_all, -3.0 * FORCE))
    idx = jnp.zeros((8, 128), F32)
    lane128 = lax.broadcasted_iota(jnp.int32, (8, 128), 1)
    lane_f = lane.astype(F32)
    for it in range(TOPK):
        m = jnp.max(score, axis=-1, keepdims=True)
        ix = jnp.min(jnp.where(score == m, lane_f, 1e6), axis=-1, keepdims=True)
        idx = jnp.where(lane128 == it, ix, idx)
        score = jnp.where(lane_f == ix, -4.0 * FORCE, score)
    idx_ref[0] = idx.astype(jnp.int32)


def _dec1(q4, ab, pos8, ov, cwin, knew4, vnew4, *, n_slc):
    nb = q4.shape[0]
    n_half = ab.shape[1]
    hg = pl.BlockSpec((1, N_KV, 8, HEAD_DIM), lambda b: (b, 0, 0, 0))
    return pl.pallas_call(
        functools.partial(_dec1_kernel, n_slc=n_slc),
        grid=(nb,),
        in_specs=[hg,
                  pl.BlockSpec((1, n_half, 2048), lambda b: (b, 0, 0)),
                  pl.BlockSpec((8, 128), lambda b: (0, 0)),
                  pl.BlockSpec(ov.shape, lambda b: (0, 0)),
                  pl.BlockSpec((1, WINDOW, 1024), lambda b: (b, 0, 0)),
                  hg, hg],
        out_specs=[hg, hg, pl.BlockSpec((1, 8, 128), lambda b: (b, 0, 0))],
        out_shape=[jax.ShapeDtypeStruct((nb, N_KV, 8, HEAD_DIM), F32),
                   jax.ShapeDtypeStruct((nb, N_KV, 8, HEAD_DIM), F32),
                   jax.ShapeDtypeStruct((nb, 8, 128), jnp.int32)],
        compiler_params=_cp(("parallel",)),
        name="dec_cmp_win",
    )(q4, ab, pos8, ov, cwin, knew4, vnew4)


def _dec2_kernel(pt_ref, si_ref, *refs, new_blk):
    del pt_ref
    kv_refs = refs[:TOPK]
    q_ref, knew_ref, vnew_ref, ocmp_ref, owin_ref, gate_ref, y_ref = refs[TOPK:]
    b = pl.program_id(0)
    g = pl.program_id(1)
    qg = (q_ref[0, 0] * SCALE).astype(BF16)
    row = lax.broadcasted_iota(jnp.int32, (SEL_BLOCK, HEAD_DIM), 0)
    lane = lax.broadcasted_iota(jnp.int32, (1, TOPK * SEL_BLOCK), 1)
    kn = knew_ref[0, 0][0:1, :]
    vn = vnew_ref[0, 0][0:1, :]
    ks, vs = [], []
    masked = jnp.zeros((1, TOPK * SEL_BLOCK), jnp.bool_)
    lane_blk = lax.shift_right_logical(lane, 6)
    lane_off = lane & (SEL_BLOCK - 1)
    for kk in range(TOPK):
        is_new = si_ref[(b * N_KV + g) * TOPK + kk] == new_blk
        sub = row == jnp.where(is_new, 0, -1)
        k_blk = kv_refs[kk][pl.ds(2 * N_KV + g, SEL_BLOCK, stride=16), :]
        v_blk = kv_refs[kk][pl.ds(3 * N_KV + g, SEL_BLOCK, stride=16), :]
        ks.append(jnp.where(sub, kn, k_blk).astype(BF16))
        vs.append(jnp.where(sub, vn, v_blk).astype(BF16))
        masked = masked | ((lane_blk == jnp.where(is_new, kk, -1)) & (lane_off > 0))
    kall = jnp.concatenate(ks, axis=0)
    vall = jnp.concatenate(vs, axis=0)
    s = jnp.where(masked, NEG, _dot_nt(qg, kall))
    e = jnp.exp(s - jnp.max(s, axis=-1, keepdims=True))
    p = e / jnp.sum(e, axis=-1, keepdims=True)
    o_sel = _dot(p.astype(BF16), vall)
    gs = _sigmoid(gate_ref[0, 0])
    y_ref[0, 0] = gs[:, 0:1] * ocmp_ref[0, 0] + gs[:, 1:2] * o_sel + gs[:, 2:3] * owin_ref[0, 0]


def _dec2(pt, sidx, cache2, q4, knew4, vnew4, ocmp, owin, gate4, *, new_blk):
    nb = q4.shape[0]
    last = new_blk - 1

    def half_page(b, g, pt, si, kk):
        j = jnp.minimum(si[(b * N_KV + g) * TOPK + kk], last)
        return pt[b, lax.shift_right_logical(j, 1)] * 2 + (j & 1)

    kv_specs = [pl.BlockSpec((SEL_BLOCK * 16, 128), lambda b, g, pt, si, kk=kk: (half_page(b, g, pt, si, kk), 0))
                for kk in range(TOPK)]
    hg = pl.BlockSpec((1, 1, 8, HEAD_DIM), lambda b, g, pt, si: (b, g, 0, 0))
    return pl.pallas_call(
        functools.partial(_dec2_kernel, new_blk=new_blk),
        grid_spec=pltpu.PrefetchScalarGridSpec(
            num_scalar_prefetch=2,
            grid=(nb, N_KV),
            in_specs=kv_specs + [hg] * 6,
            out_specs=hg),
        out_shape=jax.ShapeDtypeStruct((nb, N_KV, 8, HEAD_DIM), F32),
        compiler_params=_cp(("parallel", "parallel")),
        name="dec_sel",
    )(pt, sidx, *([cache2] * TOPK), q4, knew4, vnew4, ocmp, owin, gate4)


def _pad_rows(a, rows):
    return jnp.pad(a, ((0, rows - a.shape[0]),) + ((0, 0),) * (a.ndim - 1))


def _to_hg(a2d, nb):
    g = a2d.shape[1] // (N_KV * HEAD_DIM)
    a = a2d.reshape(nb, N_KV, g, HEAD_DIM)
    return jnp.pad(a, ((0, 0), (0, 0), (0, 8 - g), (0, 0)))


def kernel(x_prompt, x_sample, cache_kv, cache_win, cache_mem, state_conv_mix, state_conv_ffn, page_table,
           mem_prompt, ln_g, ln_b, w_in_a, conv_a_w, w_in_b, w_o, w_mem_kv, w_up, ffn_conv_w, w_down,
           w_kv_shared, cmp_pos, w_cmp):
    nb, t = x_prompt.shape[:2]
    db = x_sample.shape[0]
    m_p = nb * t
    ms = 16
    tm = 512
    tpb = t // tm
    past = page_table.shape[1] * cache_kv.shape[1]

    w_in_a_b = w_in_a.astype(BF16)
    hq = N_HEADS * HEAD_DIM
    hgate = hq + 3 * N_HEADS
    wgate = w_in_b[:, :, hq:hgate].reshape(2, D_MODEL, 3, N_KV, GROUP)
    wgate = jnp.transpose(wgate, (0, 1, 3, 2, 4)).reshape(2, D_MODEL, N_KV, 3 * GROUP)
    wgate = jnp.pad(wgate, ((0, 0), (0, 0), (0, 0), (0, 128 - 3 * GROUP))).reshape(2, D_MODEL, N_KV * 128)
    w_qg_b = jnp.concatenate([w_in_b[:, :, :hq], wgate], axis=-1).astype(BF16)
    w_qm_b = jnp.concatenate([w_in_a[:, :, 3 * D_A:], w_in_b[:, :, hgate:]], axis=0).astype(BF16)
    w_o_b = w_o.astype(BF16)
    padf = D_FF_PAD - D_FF
    w_up_z = jnp.pad(w_up[:, :, :D_FF], ((0, 0), (0, 0), (0, padf))).astype(BF16)
    w_up_g = jnp.pad(w_up[:, :, D_FF:], ((0, 0), (0, 0), (0, padf))).astype(BF16)
    w_down_b = jnp.pad(w_down, ((0, 0), (0, padf), (0, 0))).astype(BF16)
    ffn_cw = jnp.pad(ffn_conv_w, ((0, 0), (0, 0), (0, padf)))
    w_kv_b = w_kv_shared.astype(BF16)
    w_mem_b = w_mem_kv.astype(BF16)
    w_cmp2 = jnp.concatenate([w_cmp[:, :CMP_STRIDE].reshape(2, 2048, 128),
                              w_cmp[:, CMP_STRIDE:].reshape(2, 2048, 128)], axis=-1).astype(BF16)
    pos8 = _posterm(jnp.broadcast_to(cmp_pos.reshape(2, 1, CMP_LEN * HEAD_DIM), (2, 8, CMP_LEN * HEAD_DIM)),
                    w_cmp.reshape(2, CMP_LEN * HEAD_DIM, HEAD_DIM))
    g2 = ln_g.reshape(2 * DEPTH, 1, D_MODEL)
    b2 = ln_b.reshape(2 * DEPTH, 1, D_MODEL)

    def overlap(n_cmp, n_lanes_j, n_rows):
        n = jnp.arange(n_rows)[:, None]
        j = jnp.arange(n_lanes_j)[None, :]
        return ((n >= 4 * j - 1) & (n <= 4 * j + 3) & (n < n_cmp)).astype(BF16)

    ovt_p = overlap(t // CMP_STRIDE - 1, 128, 128).T
    emat = (jnp.arange(t)[:, None] // SEL_BLOCK == jnp.arange(128)[None, :]).astype(BF16)
    n_half_s = past // CMP_STRIDE
    n_slc_s = past // SEL_BLOCK + 1
    ov_s = overlap(n_half_s - 1, 384, n_half_s)

    x = x_prompt.reshape(m_p, D_MODEL)
    mem_x = mem_prompt.reshape(nb * N_MEM, D_MODEL)
    mem_kv = [_mm(mem_x, w_mem_b, layer=l, tm=512, tn=2 * MEM_W, name="mm_memkv").reshape(nb, N_MEM, 2 * MEM_W)
              for l in range(DEPTH)]
    zeros_mix = jnp.zeros((nb, 8, D_A), F32)
    zeros_ffn = jnp.zeros((nb, 8, D_FF_PAD), F32)
    conv_p, ffn_p = [], []

    def tail(x, l, y, memkv_l, seq_prev8):
        x = _oproj_mem(x, y, w_qm_b, memkv_l, w_o_b, g2, b2, layer=l, tm=tm, tpb=tpb)
        x, st = _ffn_seq(x, w_up_z, w_up_g, ffn_cw, w_down_b, seq_prev8, g2, b2,
                         layer=l, tm=tm, tf=512, tpb=tpb)
        return x, st

    for l in range(N_A):
        y, st = _amix_seq(x, w_in_a_b, conv_a_w, zeros_mix, layer=l, tm=tm, tn=512, tpb=tpb)
        conv_p.append(st[tpb - 1::tpb, 6:8])
        x, fs = tail(x, l, y, mem_kv[l], zeros_ffn)
        ffn_p.append(fs[tpb - 1::tpb, 6:8, :D_FF])

    kv_tr, kvb_rows = _kvrows(x, w_kv_b, tm=tm, tn=2048)
    kv_win, kvb_win = _mm(x, w_kv_b, tm=tm, tn=1024, n=1024, col0=2, out_dtypes=(F32, BF16), name="mm_kvwin")
    pt_p = jnp.arange(m_p // PAGE, dtype=jnp.int32).reshape(nb, t // PAGE)
    ab_p = _compress(kv_tr.reshape(m_p, 16, HEAD_DIM), pt_p, w_cmp2, npg=t // PAGE)

    for j in range(DEPTH - N_A):
        l = N_A + j
        pb = _mm(x, w_qg_b, layer=j, tm=tm, tn=2048, name="mm_qgate")
        y = _nsa_prompt(pb, kvb_rows, kvb_win, ab_p, pos8, ovt_p, emat, nb=nb, t=t, tq=256, ck=512)
        x, fs = tail(x, l, y, mem_kv[l], zeros_ffn)
        ffn_p.append(fs[tpb - 1::tpb, 6:8, :D_FF])
    y_prompt = x.reshape(nb, t, D_MODEL)

    xs = _pad_rows(x_sample.reshape(db, D_MODEL), ms)
    cmem_rows = cache_mem.reshape(DEPTH * db * N_MEM * 8, HEAD_DIM)
    prev_mix = jnp.pad(state_conv_mix.transpose(0, 2, 1, 3), ((0, 0), (0, 0), (0, ms - db), (0, 0)))
    prev_ffn = jnp.pad(state_conv_ffn.transpose(0, 2, 1, 3), ((0, 0), (0, 0), (0, ms - db), (0, padf)))
    cu_s, uz_s = [], []

    def tail_s(xs, l, y):
        xrep = jnp.broadcast_to(xs[:db, None, :], (db, 8, D_MODEL))
        mo = _pad_rows(_memattn_rows(xrep, w_qm_b, cmem_rows, layer=l, tt=8, out_dtype=F32)[:, 0], ms)
        xs = _oproj(xs, y, mo, w_o_b, g2, b2, layer=l, tm=ms)
        xs, uz = _ffn_tok(xs, w_up_z, w_up_g, ffn_cw, w_down_b, prev_ffn, g2, b2, layer=l, tf=512)
        uz_s.append(uz)
        return xs

    for l in range(N_A):
        y, cu = _amix_tok(xs, w_in_a_b, conv_a_w, prev_mix, layer=l, tn=512)
        cu_s.append(cu)
        xs = tail_s(xs, l, y)

    kv_s = _mm(xs, w_kv_b, tm=ms, tn=512, name="mm_kv_tok")[:db]
    n_tok = cache_kv.shape[0] * PAGE
    ab_s = _compress(cache_kv.reshape(n_tok, 16, HEAD_DIM), page_table, w_cmp2, npg=16)
    cache2 = cache_kv.reshape(n_tok * 16, HEAD_DIM)
    cwin = cache_win.reshape(db, WINDOW, 1024)
    ksel_new = _to_hg(kv_s[:, 1024:1536], db)
    vsel_new = _to_hg(kv_s[:, 1536:2048], db)
    kwin_new = _to_hg(kv_s[:, 2048:2560], db)
    vwin_new = _to_hg(kv_s[:, 2560:3072], db)

    for j in range(DEPTH - N_A):
        l = N_A + j
        pbs = _mm(xs, w_qg_b, layer=j, tm=ms, tn=512, name="mm_qgate_tok")[:db]
        q4 = _to_hg(pbs[:, :hq], db)
        gate4 = pbs[:, hq:].reshape(db, N_KV, 128)[:, :, :3 * GROUP].reshape(db, N_KV, 3, GROUP)
        gate4 = jnp.pad(gate4.transpose(0, 1, 3, 2), ((0, 0), (0, 0), (0, 8 - GROUP), (0, 128 - 3)))
        ocmp, owin, idx = _dec1(q4, ab_s, pos8, ov_s, cwin, kwin_new, vwin_new, n_slc=n_slc_s)
        sidx = idx[:, :N_KV, :TOPK].reshape(db * N_KV * TOPK)
        y4 = _dec2(page_table, sidx, cache2, q4, ksel_new, vsel_new, ocmp, owin, gate4, new_blk=n_slc_s - 1)
        y = _pad_rows(y4[:, :, :GROUP].reshape(db, D_A), ms).astype(BF16)
        xs = tail_s(xs, l, y)
    y_sample = xs[:db].reshape(db, 1, D_MODEL)

    wb = cache_win.shape[1]
    win_new = kv_s[:, 2048:].reshape(db, 1, 2, N_KV, HEAD_DIM)
    return (y_prompt, y_sample,
            kv_tr.reshape(nb, t, 4, N_KV, HEAD_DIM),
            kv_win.reshape(nb, t, 1024)[:, max(t - WINDOW, 0):].reshape(nb, min(t, WINDOW), 2, N_KV, HEAD_DIM),
            jnp.stack(mem_kv).reshape(DEPTH, nb, N_MEM, 2, 4, HEAD_DIM),
            jnp.stack(conv_p),
            jnp.stack(ffn_p),
            kv_s[:, :2048].reshape(db, 1, 4, N_KV, HEAD_DIM),
            jnp.concatenate([cache_win, win_new], axis=1)[:, -wb:],
            jnp.stack([state_conv_mix[:, :, 1], jnp.stack(cu_s)[:, :db]], axis=2),
            jnp.stack([state_conv_ffn[:, :, 1], jnp.stack(uz_s)[:, :db, :D_FF]], axis=2))
```

```python
import functools

import jax
import jax.numpy as jnp
from jax import lax
from jax.experimental import pallas as pl
from jax.experimental.pallas import tpu as pltpu

F32 = jnp.float32
BF16 = jnp.bfloat16

D_MODEL = 2048
DEPTH = 4
HEAD_DIM = 128
N_A = 2
D_A = 1536
N_HEADS = 12
N_KV = 4
GROUP = 3
MEM_W = 512
N_MEM = 256
D_FF = 5504
D_FF_PAD = 5632
CMP_STRIDE = 16
CMP_LEN = 32
SEL_BLOCK = 64
TOPK = 16
WINDOW = 512
PAGE = 128
ALPHA = (2 * DEPTH) ** 0.25
LN_EPS = 1e-5
NEG = -1e30
FORCE = 1e9
SCALE = HEAD_DIM ** -0.5
LOG2E = 1.4426950408889634

VMEM_LIMIT = 56 * 1024 * 1024


def _cp(sem):
    return pltpu.CompilerParams(dimension_semantics=sem, vmem_limit_bytes=VMEM_LIMIT)


def _dot(a, b):
    return jnp.dot(a, b, preferred_element_type=F32)


def _wspec(block, index_map, layer=None):
    if layer is None:
        return pl.BlockSpec(block, index_map)
    return pl.BlockSpec((None,) + block, lambda *a: (layer,) + index_map(*a))


def _dot_nt(a, b):
    return lax.dot_general(a, b, (((1,), (1,)), ((), ())), preferred_element_type=F32)


def _layer_norm(v, g, b):
    mu = jnp.mean(v, axis=-1, keepdims=True)
    d = v - mu
    var = jnp.mean(d * d, axis=-1, keepdims=True)
    return d * lax.rsqrt(var + LN_EPS) * g + b


def _sigmoid(x):
    return 1.0 / (1.0 + jnp.exp(-x))


def _fold(x, op):
    r = x[:, 0:128]
    for i in range(1, x.shape[1] // 128):
        r = op(r, x[:, i * 128:(i + 1) * 128])
    return r


def _mm_kernel(x_ref, w_ref, *rest, n_out):
    o_refs = rest[:n_out]
    xb_ref = rest[n_out]

    @pl.when(pl.program_id(1) == 0)
    def _():
        xb_ref[...] = x_ref[...].astype(BF16)

    r = _dot(xb_ref[...], w_ref[...])
    for o in o_refs:
        o[...] = r.astype(o.dtype)


def _mm(x, w, *, tm, tn, n=None, col0=0, layer=None, out_dtypes=(F32,), name="mm"):
    m, k = x.shape
    n = w.shape[-1] if n is None else n
    outs = pl.pallas_call(
        functools.partial(_mm_kernel, n_out=len(out_dtypes)),
        grid=(m // tm, n // tn),
        in_specs=[pl.BlockSpec((tm, k), lambda i, j: (i, 0)),
                  _wspec((k, tn), lambda i, j: (0, col0 + j), layer)],
        out_specs=[pl.BlockSpec((tm, tn), lambda i, j: (i, j)) for _ in out_dtypes],
        out_shape=[jax.ShapeDtypeStruct((m, n), d) for d in out_dtypes],
        scratch_shapes=[pltpu.VMEM((tm, k), BF16)],
        compiler_params=_cp(("parallel", "arbitrary")),
        name=name,
    )(x, w)
    return outs if len(out_dtypes) > 1 else outs[0]


def _kvrows_kernel(x_ref, w_ref, tr_ref, ob_ref, xb_ref):
    j = pl.program_id(1)

    @pl.when(j == 0)
    def _():
        xb_ref[...] = x_ref[...].astype(BF16)

    r = _dot(xb_ref[...], w_ref[...])
    ob_ref[...] = r.astype(BF16)
    tm = r.shape[0]
    for c in range(r.shape[1] // HEAD_DIM):
        tr_ref[pl.ds(j * (r.shape[1] // HEAD_DIM) + c, tm, stride=16), :] = r[:, c * HEAD_DIM:(c + 1) * HEAD_DIM]


def _kvrows(x, w, *, tm, tn):
    m, k = x.shape
    n = 16 * HEAD_DIM
    return pl.pallas_call(
        _kvrows_kernel,
        grid=(m // tm, n // tn),
        in_specs=[pl.BlockSpec((tm, k), lambda i, j: (i, 0)),
                  pl.BlockSpec((k, tn), lambda i, j: (0, j))],
        out_specs=[pl.BlockSpec((tm * 16, HEAD_DIM), lambda i, j: (i, 0)),
                   pl.BlockSpec((tm, tn), lambda i, j: (i, j))],
        out_shape=[jax.ShapeDtypeStruct((m * 16, HEAD_DIM), F32),
                   jax.ShapeDtypeStruct((m, n), BF16)],
        scratch_shapes=[pltpu.VMEM((tm, k), BF16)],
        compiler_params=_cp(("parallel", "arbitrary")),
        name="kvrows",
    )(x, w)


def _conv_rows(v, cw_ref, tail8):
    row = lax.broadcasted_iota(jnp.int32, v.shape, 0)
    t0 = tail8[6:7, :]
    t1 = tail8[7:8, :]
    s1 = jnp.where(row == 0, t1, pltpu.roll(v, 1, 0))
    s2 = jnp.where(row == 0, t0, jnp.where(row == 1, t1, pltpu.roll(v, 2, 0)))
    return cw_ref[0:1, :] * s2 + cw_ref[1:2, :] * s1 + cw_ref[2:3, :] * v


def _amix_seq_kernel(x_ref, wu_ref, wb_ref, wc_ref, cw_ref, prev_ref, y_ref, st_ref,
                     xb_ref, carry_ref, *, tpb):
    i = pl.program_id(0)
    j = pl.program_id(1)

    @pl.when(j == 0)
    def _():
        xb_ref[...] = x_ref[...].astype(BF16)

    xb = xb_ref[...]
    cu = _dot(xb, wc_ref[...]) * _dot(xb, wu_ref[...])
    tail8 = jnp.where((i % tpb) == 0, prev_ref[0], carry_ref[j])
    conv = _conv_rows(cu, cw_ref, tail8)
    y_ref[...] = (_dot(xb, wb_ref[...]) * conv).astype(y_ref.dtype)
    last8 = cu[cu.shape[0] - 8:, :]
    carry_ref[j] = last8
    st_ref[0] = last8


def _amix_seq(x, w_in, cw, prev8, *, layer, tm, tn, tpb):
    m = x.shape[0]
    nj = D_A // tn
    return pl.pallas_call(
        functools.partial(_amix_seq_kernel, tpb=tpb),
        grid=(m // tm, nj),
        in_specs=[pl.BlockSpec((tm, D_MODEL), lambda i, j: (i, 0)),
                  _wspec((D_MODEL, tn), lambda i, j: (0, j), layer),
                  _wspec((D_MODEL, tn), lambda i, j: (0, nj + j), layer),
                  _wspec((D_MODEL, tn), lambda i, j: (0, 2 * nj + j), layer),
                  _wspec((3, tn), lambda i, j: (0, j), layer),
                  pl.BlockSpec((1, 8, tn), lambda i, j: (i // tpb, 0, j))],
        out_specs=[pl.BlockSpec((tm, tn), lambda i, j: (i, j)),
                   pl.BlockSpec((1, 8, tn), lambda i, j: (i, 0, j))],
        out_shape=[jax.ShapeDtypeStruct((m, D_A), BF16),
                   jax.ShapeDtypeStruct((m // tm, 8, D_A), F32)],
        scratch_shapes=[pltpu.VMEM((tm, D_MODEL), BF16),
                        pltpu.VMEM((nj, 8, tn), F32)],
        compiler_params=_cp(("arbitrary", "arbitrary")),
        name="amix_seq",
    )(x, w_in, w_in, w_in, cw, prev8)


def _amix_tok_kernel(x_ref, wu_ref, wb_ref, wc_ref, cw_ref, p0_ref, p1_ref, y_ref, cu_ref):
    xb = x_ref[...].astype(BF16)
    cu = _dot(xb, wc_ref[...]) * _dot(xb, wu_ref[...])
    conv = cw_ref[0:1, :] * p0_ref[...] + cw_ref[1:2, :] * p1_ref[...] + cw_ref[2:3, :] * cu
    y_ref[...] = (_dot(xb, wb_ref[...]) * conv).astype(y_ref.dtype)
    cu_ref[...] = cu


def _amix_tok(x, w_in, cw, prev, *, layer, tn):
    m = x.shape[0]
    nj = D_A // tn
    return pl.pallas_call(
        _amix_tok_kernel,
        grid=(nj,),
        in_specs=[pl.BlockSpec((m, D_MODEL), lambda j: (0, 0)),
                  _wspec((D_MODEL, tn), lambda j: (0, j), layer),
                  _wspec((D_MODEL, tn), lambda j: (0, nj + j), layer),
                  _wspec((D_MODEL, tn), lambda j: (0, 2 * nj + j), layer),
                  _wspec((3, tn), lambda j: (0, j), layer),
                  pl.BlockSpec((None, None, m, tn), lambda j: (layer, 0, 0, j)),
                  pl.BlockSpec((None, None, m, tn), lambda j: (layer, 1, 0, j))],
        out_specs=[pl.BlockSpec((m, tn), lambda j: (0, j)),
                   pl.BlockSpec((m, tn), lambda j: (0, j))],
        out_shape=[jax.ShapeDtypeStruct((m, D_A), BF16),
                   jax.ShapeDtypeStruct((m, D_A), F32)],
        compiler_params=_cp(("parallel",)),
        name="amix_tok",
    )(x, w_in, w_in, w_in, cw, prev, prev)


def _memattn_body(x_ref, wq_ref, o_ref, key, val):
    xb = x_ref[0].astype(BF16)
    qm = _dot(xb, wq_ref[...]) * SCALE
    for h in range(4):
        sl = slice(h * HEAD_DIM, (h + 1) * HEAD_DIM)
        q = qm[:, sl].astype(BF16)
        s = _dot_nt(q, key(h).astype(BF16))
        e = jnp.exp(s - jnp.max(s, axis=-1, keepdims=True))
        p = e / jnp.sum(e, axis=-1, keepdims=True)
        o_ref[0, :, sl] = _dot(p.astype(BF16), val(h).astype(BF16)).astype(o_ref.dtype)


def _memattn_rows_kernel(x_ref, wq_ref, mkv_ref, o_ref):
    _memattn_body(x_ref, wq_ref, o_ref,
                  lambda h: mkv_ref[pl.ds(h, N_MEM, stride=8), :],
                  lambda h: mkv_ref[pl.ds(4 + h, N_MEM, stride=8), :])


def _memattn_rows(x3, wq, memrows, *, layer, tt, out_dtype):
    nb, t, _ = x3.shape
    return pl.pallas_call(
        _memattn_rows_kernel,
        grid=(nb, t // tt),
        in_specs=[pl.BlockSpec((1, tt, D_MODEL), lambda b, i: (b, i, 0)),
                  _wspec((D_MODEL, MEM_W), lambda b, i: (0, 0), layer),
                  pl.BlockSpec((N_MEM * 8, HEAD_DIM), lambda b, i: (layer * nb + b, 0))],
        out_specs=pl.BlockSpec((1, tt, MEM_W), lambda b, i: (b, i, 0)),
        out_shape=jax.ShapeDtypeStruct((nb, t, MEM_W), out_dtype),
        compiler_params=_cp(("parallel", "parallel")),
        name="memattn_rows",
    )(x3, wq, memrows)


def _oproj_kernel(x_ref, y_ref, m_ref, wy_ref, wm_ref, g_ref, b_ref, o_ref):
    acc = _dot(y_ref[...].astype(BF16), wy_ref[...]) + _dot(m_ref[...].astype(BF16), wm_ref[...])
    o_ref[...] = _layer_norm(ALPHA * x_ref[...] + acc, g_ref[...], b_ref[...])


def _oproj(x, y, mo, wo, g, b, *, layer, tm):
    gb = 2 * layer
    m = x.shape[0]
    return pl.pallas_call(
        _oproj_kernel,
        grid=(m // tm,),
        in_specs=[pl.BlockSpec((tm, D_MODEL), lambda i: (i, 0)),
                  pl.BlockSpec((tm, D_A), lambda i: (i, 0)),
                  pl.BlockSpec((tm, MEM_W), lambda i: (i, 0)),
                  _wspec((D_A, D_MODEL), lambda i: (0, 0), layer),
                  _wspec((MEM_W, D_MODEL), lambda i: (D_A // MEM_W, 0), layer),
                  _wspec((1, D_MODEL), lambda i: (0, 0), gb),
                  _wspec((1, D_MODEL), lambda i: (0, 0), gb)],
        out_specs=pl.BlockSpec((tm, D_MODEL), lambda i: (i, 0)),
        out_shape=jax.ShapeDtypeStruct((m, D_MODEL), F32),
        compiler_params=_cp(("parallel",)),
        name="oproj_ln",
    )(x, y, mo, wo, wo, g, b)


def _oproj_mem_kernel(x_ref, y_ref, wq_ref, mk_ref, mv_ref, wy_ref, wm_ref, g_ref, b_ref, o_ref):
    x = x_ref[...]
    qm = _dot(x.astype(BF16), wq_ref[...]) * SCALE
    acc = _dot(y_ref[...].astype(BF16), wy_ref[...])
    mo = []
    for h in range(4):
        sl = slice(h * HEAD_DIM, (h + 1) * HEAD_DIM)
        s = _dot_nt(qm[:, sl].astype(BF16), mk_ref[0, :, sl].astype(BF16))
        e = jnp.exp(s - jnp.max(s, axis=-1, keepdims=True))
        p = e / jnp.sum(e, axis=-1, keepdims=True)
        mo.append(_dot(p.astype(BF16), mv_ref[0, :, sl].astype(BF16)).astype(BF16))
    acc = acc + _dot(jnp.concatenate(mo, axis=1), wm_ref[...])
    o_ref[...] = _layer_norm(ALPHA * x + acc, g_ref[...], b_ref[...])


def _oproj_mem(x, y, wq, memkv, wo, g, b, *, layer, tm, tpb):
    gb = 2 * layer
    m = x.shape[0]
    return pl.pallas_call(
        _oproj_mem_kernel,
        grid=(m // tm,),
        in_specs=[pl.BlockSpec((tm, D_MODEL), lambda i: (i, 0)),
                  pl.BlockSpec((tm, D_A), lambda i: (i, 0)),
                  _wspec((D_MODEL, MEM_W), lambda i: (0, 0), layer),
                  pl.BlockSpec((1, N_MEM, MEM_W), lambda i: (i // tpb, 0, 0)),
                  pl.BlockSpec((1, N_MEM, MEM_W), lambda i: (i // tpb, 0, 1)),
                  _wspec((D_A, D_MODEL), lambda i: (0, 0), layer),
                  _wspec((MEM_W, D_MODEL), lambda i: (D_A // MEM_W, 0), layer),
                  _wspec((1, D_MODEL), lambda i: (0, 0), gb),
                  _wspec((1, D_MODEL), lambda i: (0, 0), gb)],
        out_specs=pl.BlockSpec((tm, D_MODEL), lambda i: (i, 0)),
        out_shape=jax.ShapeDtypeStruct((m, D_MODEL), F32),
        compiler_params=_cp(("parallel",)),
        name="oproj_mem_ln",
    )(x, y, wq, memkv, memkv, wo, wo, g, b)


def _ffn_seq_kernel(x_ref, wz_ref, wg_ref, cw_ref, wd_ref, prev_ref, g_ref, b_ref,
                    o_ref, st_ref, xb_ref, acc_ref, carry_ref, *, tpb):
    i = pl.program_id(0)
    j = pl.program_id(1)

    @pl.when(j == 0)
    def _():
        xb_ref[...] = x_ref[...].astype(BF16)
        acc_ref[...] = jnp.zeros_like(acc_ref)

    xb = xb_ref[...]
    uz = _dot(xb, wz_ref[...])
    tail8 = jnp.where((i % tpb) == 0, prev_ref[0], carry_ref[j])
    z = _conv_rows(uz, cw_ref, tail8)
    h = z * _sigmoid(z) * _dot(xb, wg_ref[...])
    acc_ref[...] += _dot(h.astype(BF16), wd_ref[...])
    last8 = uz[uz.shape[0] - 8:, :]
    carry_ref[j] = last8
    st_ref[0] = last8

    @pl.when(j == pl.num_programs(1) - 1)
    def _():
        o_ref[...] = _layer_norm(ALPHA * x_ref[...] + acc_ref[...], g_ref[...], b_ref[...])


def _ffn_seq(x, wz, wg, cw, wd, prev8, g, b, *, layer, tm, tf, tpb):
    gb = 2 * layer + 1
    m = x.shape[0]
    nj = D_FF_PAD // tf
    return pl.pallas_call(
        functools.partial(_ffn_seq_kernel, tpb=tpb),
        grid=(m // tm, nj),
        in_specs=[pl.BlockSpec((tm, D_MODEL), lambda i, j: (i, 0)),
                  _wspec((D_MODEL, tf), lambda i, j: (0, j), layer),
                  _wspec((D_MODEL, tf), lambda i, j: (0, j), layer),
                  _wspec((3, tf), lambda i, j: (0, j), layer),
                  _wspec((tf, D_MODEL), lambda i, j: (j, 0), layer),
                  pl.BlockSpec((1, 8, tf), lambda i, j: (i // tpb, 0, j)),
                  _wspec((1, D_MODEL), lambda i, j: (0, 0), gb),
                  _wspec((1, D_MODEL), lambda i, j: (0, 0), gb)],
        out_specs=[pl.BlockSpec((tm, D_MODEL), lambda i, j: (i, 0)),
                   pl.BlockSpec((1, 8, tf), lambda i, j: (i, 0, j))],
        out_shape=[jax.ShapeDtypeStruct((m, D_MODEL), F32),
                   jax.ShapeDtypeStruct((m // tm, 8, D_FF_PAD), F32)],
        scratch_shapes=[pltpu.VMEM((tm, D_MODEL), BF16),
                        pltpu.VMEM((tm, D_MODEL), F32),
                        pltpu.VMEM((nj, 8, tf), F32)],
        compiler_params=_cp(("arbitrary", "arbitrary")),
        name="ffn_seq",
    )(x, wz, wg, cw, wd, prev8, g, b)


def _ffn_tok_kernel(x_ref, wz_ref, wg_ref, cw_ref, wd_ref, p0_ref, p1_ref, g_ref, b_ref,
                    o_ref, uz_ref, acc_ref):
    j = pl.program_id(0)

    @pl.when(j == 0)
    def _():
        acc_ref[...] = jnp.zeros_like(acc_ref)

    xb = x_ref[...].astype(BF16)
    uz = _dot(xb, wz_ref[...])
    z = cw_ref[0:1, :] * p0_ref[...] + cw_ref[1:2, :] * p1_ref[...] + cw_ref[2:3, :] * uz
    h = z * _sigmoid(z) * _dot(xb, wg_ref[...])
    acc_ref[...] += _dot(h.astype(BF16), wd_ref[...])
    uz_ref[...] = uz

    @pl.when(j == pl.num_programs(0) - 1)
    def _():
        o_ref[...] = _layer_norm(ALPHA * x_ref[...] + acc_ref[...], g_ref[...], b_ref[...])


def _ffn_tok(x, wz, wg, cw, wd, prev, g, b, *, layer, tf):
    gb = 2 * layer + 1
    m = x.shape[0]
    nj = D_FF_PAD // tf
    return pl.pallas_call(
        _ffn_tok_kernel,
        grid=(nj,),
        in_specs=[pl.BlockSpec((m, D_MODEL), lambda j: (0, 0)),
                  _wspec((D_MODEL, tf), lambda j: (0, j), layer),
                  _wspec((D_MODEL, tf), lambda j: (0, j), layer),
                  _wspec((3, tf), lambda j: (0, j), layer),
                  _wspec((tf, D_MODEL), lambda j: (j, 0), layer),
                  pl.BlockSpec((None, None, m, tf), lambda j: (layer, 0, 0, j)),
                  pl.BlockSpec((None, None, m, tf), lambda j: (layer, 1, 0, j)),
                  _wspec((1, D_MODEL), lambda j: (0, 0), gb),
                  _wspec((1, D_MODEL), lambda j: (0, 0), gb)],
        out_specs=[pl.BlockSpec((m, D_MODEL), lambda j: (0, 0)),
                   pl.BlockSpec((m, tf), lambda j: (0, j))],
        out_shape=[jax.ShapeDtypeStruct((m, D_MODEL), F32),
                   jax.ShapeDtypeStruct((m, D_FF_PAD), F32)],
        scratch_shapes=[pltpu.VMEM((m, D_MODEL), F32)],
        compiler_params=_cp(("arbitrary",)),
        name="ffn_tok",
    )(x, wz, wg, cw, wd, prev, prev, g, b)


def _compress_kernel(pt_ref, *refs, npg):
    del pt_ref
    page_refs = refs[:npg]
    w_ref, o_ref, x2_ref = refs[npg:]
    nh = npg * 8
    for pg in range(npg):
        for r in range(PAGE):
            n = pg * 8 + r // CMP_STRIDE
            x2_ref[r % CMP_STRIDE, n * 8:n * 8 + 8, :] = page_refs[pg][r]
    for c in range(2):
        xs = [jnp.concatenate([x2_ref[s, pl.ds(c * N_KV + g, nh, stride=8), :] for s in range(CMP_STRIDE)], axis=1)
              for g in range(N_KV)]
        r = _dot(jnp.concatenate(xs, axis=0).astype(BF16), w_ref[c])
        for g in range(N_KV):
            o_ref[0, :, c * 1024 + g * 128:c * 1024 + (g + 1) * 128] = r[g * nh:(g + 1) * nh, 0:128]
            o_ref[0, :, c * 1024 + 512 + g * 128:c * 1024 + 512 + (g + 1) * 128] = r[g * nh:(g + 1) * nh, 128:256]


def _compress(rows3, pt, w2, *, npg):
    nb, npages = pt.shape
    nh = npg * 8
    page_specs = [pl.BlockSpec((PAGE, 8, 128), lambda b, s, pt, k=k: (pt[b, s * npg + k], 0, 0))
                  for k in range(npg)]
    return pl.pallas_call(
        functools.partial(_compress_kernel, npg=npg),
        grid_spec=pltpu.PrefetchScalarGridSpec(
            num_scalar_prefetch=1,
            grid=(nb, npages // npg),
            in_specs=page_specs + [pl.BlockSpec((2, 2048, 256), lambda b, s, pt: (0, 0, 0))],
            out_specs=pl.BlockSpec((1, nh, 2048), lambda b, s, pt: (b, s, 0)),
            scratch_shapes=[pltpu.VMEM((CMP_STRIDE, nh * 8, 128), F32)]),
        out_shape=jax.ShapeDtypeStruct((nb, npages * 8, 2048), F32),
        compiler_params=_cp(("parallel", "parallel")),
        name="compress",
    )(pt, *([rows3] * npg), w2)


def _posterm_kernel(pos_ref, w_ref, o_ref):
    o_ref[...] = jnp.zeros_like(o_ref)
    for c in range(2):
        x = pos_ref[c]
        w = w_ref[c]
        xh = x.astype(BF16)
        xl = (x - xh.astype(F32)).astype(BF16)
        wh = w.astype(BF16)
        wl = (w - wh.astype(F32)).astype(BF16)
        r = _dot(xh, wh) + _dot(xh, wl) + _dot(xl, wh)
        o_ref[c:c + 1, :] = r[0:1, :]


def _posterm(pos8, w2):
    return pl.pallas_call(
        _posterm_kernel,
        out_shape=jax.ShapeDtypeStruct((8, HEAD_DIM), F32),
        compiler_params=pltpu.CompilerParams(vmem_limit_bytes=VMEM_LIMIT),
        name="posterm",
    )(pos8, w2)


def _nsa_kernel(q_ref, gate_ref, ksel_ref, vsel_ref, kwin_ref, vwin_ref,
                ak_ref, bk_ref, av_ref, bv_ref, pos_ref, ovt_ref, e_ref,
                y_ref, kcat_ref, cmpk_ref, cmpv_ref, bias_ref, *, tq, ck):
    qi = pl.program_id(2)
    t0 = pl.multiple_of(qi * tq, tq)
    n_half = ak_ref.shape[1]
    heads = lambda a: [a[r * tq:(r + 1) * tq] for r in range(GROUP)]

    @pl.when(qi == 0)
    def _():
        kcat_ref[:, 0:HEAD_DIM] = ksel_ref[...]
        kcat_ref[:, HEAD_DIM:2 * HEAD_DIM] = e_ref[...]
        cmpk_ref[...] = (ak_ref[0] + pltpu.roll(bk_ref[0], n_half - 1, 0) + pos_ref[0:1, :]).astype(BF16)
        cmpv_ref[...] = (av_ref[0] + pltpu.roll(bv_ref[0], n_half - 1, 0) + pos_ref[1:2, :]).astype(BF16)

    q = q_ref[...] * (SCALE * LOG2E)
    qs = jnp.concatenate([q[:, r * HEAD_DIM:(r + 1) * HEAD_DIM] for r in range(GROUP)], axis=0).astype(BF16)
    tcol = t0 + lax.broadcasted_iota(jnp.int32, (tq, 1), 0)

    n_l = lax.broadcasted_iota(jnp.int32, (1, n_half), 1)
    cvalid = (n_l * CMP_STRIDE + (CMP_LEN - 1) <= tcol) & (n_l < n_half - 1)
    cbias = jnp.where(cvalid, 0.0, NEG)
    ckeep = jnp.where(cvalid, 1.0, 0.0)
    s = _dot_nt(qs, cmpk_ref[...])
    s = jnp.concatenate([sh + cbias for sh in heads(s)], axis=0)
    e = jnp.exp2(s - jnp.max(s, axis=-1, keepdims=True))
    e = jnp.concatenate([eh * ckeep for eh in heads(e)], axis=0)
    l = jnp.sum(e, axis=-1, keepdims=True)
    p = e / jnp.where(l > 0.0, l, 1.0)
    o_cmp = _dot(p.astype(BF16), cmpv_ref[...])

    sub = 128
    wlen = WINDOW + sub
    win_parts = []
    for r0 in range(0, tq, sub):
        ws = pl.multiple_of(jnp.maximum(t0 + r0 - WINDOW, 0), sub)
        kpos = ws + lax.broadcasted_iota(jnp.int32, (1, wlen), 1)
        tsub = tcol[r0:r0 + sub]
        wbias = jnp.where((kpos <= tsub) & (kpos > tsub - WINDOW), 0.0, NEG)
        qsub = jnp.concatenate([qh[r0:r0 + sub] for qh in heads(qs)], axis=0)
        sw = _dot_nt(qsub, kwin_ref[pl.ds(ws, wlen), :])
        sw = jnp.concatenate([sw[r * sub:(r + 1) * sub] + wbias for r in range(GROUP)], axis=0)
        ew = jnp.exp2(sw - jnp.max(_fold(sw, jnp.maximum), axis=-1, keepdims=True))
        pw = ew / jnp.sum(_fold(ew, jnp.add), axis=-1, keepdims=True)
        win_parts.append(_dot(pw.astype(BF16), vwin_ref[pl.ds(ws, wlen), :]))
    o_win = jnp.concatenate([part[r * sub:(r + 1) * sub] for r in range(GROUP) for part in win_parts], axis=0)

    jj = lax.broadcasted_iota(jnp.int32, (32, tq), 0)
    tl = t0 + lax.broadcasted_iota(jnp.int32, (32, tq), 1)
    elig = jj * SEL_BLOCK <= tl

    def store_bias(sel_mask):
        bias_t = jnp.concatenate([jnp.where(sel_mask, 0.0, NEG), jnp.zeros((96, tq), F32)], axis=0)
        bias_ref[...] = bias_t.T.astype(BF16)

    @pl.when(t0 + tq <= TOPK * SEL_BLOCK)
    def _():
        store_bias(elig)

    @pl.when(t0 + tq > TOPK * SEL_BLOCK)
    def _():
        ps = p[0:tq] + p[tq:2 * tq] + p[2 * tq:3 * tq]
        ph = ps.astype(BF16)
        plo = (ps - ph.astype(F32)).astype(BF16)
        imp = (_dot_nt(ovt_ref[...], ph) + _dot_nt(ovt_ref[...], plo))[0:32, :]
        cur = lax.shift_right_logical(tl, 6)
        forced = (jj == 0) | (jj == cur) | (jj == cur - 1)
        score = jnp.where(forced, FORCE, jnp.where(elig, imp, -FORCE))
        sel = jnp.zeros((32, tq), F32)
        for k in range(32):
            row = score[k:k + 1, :]
            beats = (score > row) | ((score == row) & (jj < k))
            cnt = jnp.sum(beats.astype(F32), axis=0, keepdims=True)
            sel = jnp.where(jj == k, (cnt < float(TOPK)).astype(F32), sel)
        store_bias(sel > 0.0)

    qcat = jnp.concatenate([qs, jnp.concatenate([bias_ref[...]] * GROUP, axis=0)], axis=1)
    tcol3 = jnp.concatenate([tcol] * GROUP, axis=0)

    def body(c, carry):
        m, l, acc = carry
        k0 = pl.multiple_of(c * ck, ck)
        sc = _dot_nt(qcat, kcat_ref[pl.ds(k0, ck), :])
        kpos = k0 + lax.broadcasted_iota(jnp.int32, (1, ck), 1)
        sc = jnp.where(kpos <= tcol3, sc, NEG)
        mn = jnp.maximum(m, jnp.max(sc, axis=-1, keepdims=True))
        a = jnp.exp2(m - mn)
        pr = jnp.exp2(sc - mn)
        l = a * l + jnp.sum(pr, axis=-1, keepdims=True)
        acc = a * acc + _dot(pr.astype(BF16), vsel_ref[pl.ds(k0, ck), :])
        return mn, l, acc

    nch = (t0 + tq + ck - 1) // ck
    m0 = jnp.full((GROUP * tq, 1), NEG, F32)
    l0 = jnp.zeros((GROUP * tq, 1), F32)
    a0 = jnp.zeros((GROUP * tq, HEAD_DIM), F32)
    _, l, acc = lax.fori_loop(0, nch, body, (m0, l0, a0))
    o_sel = acc / l

    gs = _sigmoid(gate_ref[...])
    for r in range(GROUP):
        rs = slice(r * tq, (r + 1) * tq)
        o = (gs[:, r:r + 1] * o_cmp[rs] + gs[:, GROUP + r:GROUP + r + 1] * o_sel[rs]
             + gs[:, 2 * GROUP + r:2 * GROUP + r + 1] * o_win[rs])
        y_ref[:, r * HEAD_DIM:(r + 1) * HEAD_DIM] = o.astype(y_ref.dtype)


def _nsa_prompt(pb, kvb_rows, kvb_win, ab, pos8, ovt, emat, *, nb, t, tq, ck):
    nq = t // tq
    gw = GROUP * HEAD_DIM
    n_half = ab.shape[1]
    return pl.pallas_call(
        functools.partial(_nsa_kernel, tq=tq, ck=ck),
        grid=(nb, N_KV, nq),
        in_specs=[pl.BlockSpec((tq, gw), lambda b, g, i: (b * nq + i, g)),
                  pl.BlockSpec((tq, 128), lambda b, g, i: (b * nq + i, D_A // 128 + g)),
                  pl.BlockSpec((t, 128), lambda b, g, i: (b, 8 + g)),
                  pl.BlockSpec((t, 128), lambda b, g, i: (b, 12 + g)),
                  pl.BlockSpec((t, 128), lambda b, g, i: (b, g)),
                  pl.BlockSpec((t, 128), lambda b, g, i: (b, 4 + g)),
                  pl.BlockSpec((1, n_half, 128), lambda b, g, i: (b, 0, g)),
                  pl.BlockSpec((1, n_half, 128), lambda b, g, i: (b, 0, 4 + g)),
                  pl.BlockSpec((1, n_half, 128), lambda b, g, i: (b, 0, 8 + g)),
                  pl.BlockSpec((1, n_half, 128), lambda b, g, i: (b, 0, 12 + g)),
                  pl.BlockSpec((8, 128), lambda b, g, i: (0, 0)),
                  pl.BlockSpec((128, 128), lambda b, g, i: (0, 0)),
                  pl.BlockSpec((t, 128), lambda b, g, i: (0, 0))],
        out_specs=pl.BlockSpec((tq, gw), lambda b, g, i: (b * nq + i, g)),
        out_shape=jax.ShapeDtypeStruct((nb * t, D_A), BF16),
        scratch_shapes=[pltpu.VMEM((t, 2 * HEAD_DIM), BF16),
                        pltpu.VMEM((n_half, HEAD_DIM), BF16),
                        pltpu.VMEM((n_half, HEAD_DIM), BF16),
                        pltpu.VMEM((tq, 128), BF16)],
        compiler_params=_cp(("parallel", "parallel", "arbitrary")),
        name="nsa_prompt",
    )(pb, pb, kvb_rows, kvb_rows, kvb_win, kvb_win, ab, ab, ab, ab, pos8, ovt, emat)


def _dec1_kernel(q_ref, ab_ref, pos_ref, ov_ref, cwin_ref, knew_ref, vnew_ref,
                 ocmp_ref, owin_ref, idx_ref, *, n_slc):
    n_half = ab_ref.shape[1]
    nlane = ov_ref.shape[1]
    lane = lax.broadcasted_iota(jnp.int32, (8, nlane), 1)
    row8 = lax.broadcasted_iota(jnp.int32, (8, nlane), 0)
    imp_all = jnp.zeros((8, nlane), F32)
    for g in range(N_KV):
        sl = slice(g * HEAD_DIM, (g + 1) * HEAD_DIM)
        qg = (q_ref[0, g] * SCALE).astype(BF16)
        cmpk = (ab_ref[0, :, sl] + pltpu.roll(ab_ref[0, :, 512 + g * 128:512 + (g + 1) * 128], n_half - 1, 0)
                + pos_ref[0:1, :]).astype(BF16)
        cmpv = (ab_ref[0, :, 1024 + g * 128:1024 + (g + 1) * 128]
                + pltpu.roll(ab_ref[0, :, 1536 + g * 128:1536 + (g + 1) * 128], n_half - 1, 0)
                + pos_ref[1:2, :]).astype(BF16)
        n_l = lax.broadcasted_iota(jnp.int32, (1, n_half), 1)
        cm = n_l < n_half - 1
        s = jnp.where(cm, _dot_nt(qg, cmpk), NEG)
        e = jnp.where(cm, jnp.exp(s - jnp.max(s, axis=-1, keepdims=True)), 0.0)
        p = e / jnp.sum(e, axis=-1, keepdims=True)
        ocmp_ref[0, g] = _dot(p.astype(BF16), cmpv)
        ps = p[0:1] + p[1:2] + p[2:3]
        ps8 = jnp.broadcast_to(ps, (8, n_half))
        ph = ps8.astype(BF16)
        plo = (ps8 - ph.astype(F32)).astype(BF16)
        imp = _dot(ph, ov_ref[...]) + _dot(plo, ov_ref[...])
        imp_all = jnp.where(row8 == g, imp, imp_all)

        kc = cwin_ref[0, :, sl].astype(BF16)
        vc = cwin_ref[0, :, 512 + g * 128:512 + (g + 1) * 128].astype(BF16)
        kn = knew_ref[0, g][0:1, :].astype(BF16).astype(F32)
        vn = vnew_ref[0, g][0:1, :].astype(BF16).astype(F32)
        w_l = lax.broadcasted_iota(jnp.int32, (1, kc.shape[0]), 1)
        sc = jnp.where(w_l >= 1, _dot_nt(qg, kc), NEG)
        sn = jnp.sum(qg.astype(F32) * kn, axis=-1, keepdims=True)
        m = jnp.maximum(jnp.max(sc, axis=-1, keepdims=True), sn)
        ec = jnp.exp(sc - m)
        en = jnp.exp(sn - m)
        lw = jnp.sum(ec, axis=-1, keepdims=True) + en
        owin_ref[0, g] = _dot((ec / lw).astype(BF16), vc) + (en / lw).astype(BF16).astype(F32) * vn

    cur = n_slc - 1
    forced = (lane == 0) | (lane == cur) | (lane == cur - 1)
    score = jnp.where(forced, FORCE, jnp.where(lane < n_slc, imp_all, -3.0 * FORCE))
    idx = jnp.zeros((8, 128), F32)
    lane128 = lax.broadcasted_iota(jnp.int32, (8, 128), 1)
    lane_f = lane.astype(F32)
    for it in range(TOPK):
        m = jnp.max(score, axis=-1, keepdims=True)
        ix = jnp.min(jnp.where(score == m, lane_f, 1e6), axis=-1, keepdims=True)
        idx = jnp.where(lane128 == it, ix, idx)
        score = jnp.where(lane_f == ix, -4.0 * FORCE, score)
    idx_ref[0] = idx.astype(jnp.int32)


def _dec1(q4, ab, pos8, ov, cwin, knew4, vnew4, *, n_slc):
    nb = q4.shape[0]
    n_half = ab.shape[1]
    hg = pl.BlockSpec((1, N_KV, 8, HEAD_DIM), lambda b: (b, 0, 0, 0))
    return pl.pallas_call(
        functools.partial(_dec1_kernel, n_slc=n_slc),
        grid=(nb,),
        in_specs=[hg,
                  pl.BlockSpec((1, n_half, 2048), lambda b: (b, 0, 0)),
                  pl.BlockSpec((8, 128), lambda b: (0, 0)),
                  pl.BlockSpec(ov.shape, lambda b: (0, 0)),
                  pl.BlockSpec((1, WINDOW, 1024), lambda b: (b, 0, 0)),
                  hg, hg],
        out_specs=[hg, hg, pl.BlockSpec((1, 8, 128), lambda b: (b, 0, 0))],
        out_shape=[jax.ShapeDtypeStruct((nb, N_KV, 8, HEAD_DIM), F32),
                   jax.ShapeDtypeStruct((nb, N_KV, 8, HEAD_DIM), F32),
                   jax.ShapeDtypeStruct((nb, 8, 128), jnp.int32)],
        compiler_params=_cp(("parallel",)),
        name="dec_cmp_win",
    )(q4, ab, pos8, ov, cwin, knew4, vnew4)


def _dec2_kernel(pt_ref, si_ref, *refs, new_blk):
    del pt_ref
    kv_refs = refs[:TOPK]
    q_ref, knew_ref, vnew_ref, ocmp_ref, owin_ref, gate_ref, y_ref = refs[TOPK:]
    b = pl.program_id(0)
    g = pl.program_id(1)
    qg = (q_ref[0, 0] * SCALE).astype(BF16)
    row = lax.broadcasted_iota(jnp.int32, (SEL_BLOCK, HEAD_DIM), 0)
    lane = lax.broadcasted_iota(jnp.int32, (1, TOPK * SEL_BLOCK), 1)
    kn = knew_ref[0, 0][0:1, :]
    vn = vnew_ref[0, 0][0:1, :]
    ks, vs = [], []
    masked = jnp.zeros((1, TOPK * SEL_BLOCK), jnp.bool_)
    lane_blk = lax.shift_right_logical(lane, 6)
    lane_off = lane & (SEL_BLOCK - 1)
    for kk in range(TOPK):
        is_new = si_ref[(b * N_KV + g) * TOPK + kk] == new_blk
        sub = row == jnp.where(is_new, 0, -1)
        k_blk = kv_refs[kk][pl.ds(2 * N_KV + g, SEL_BLOCK, stride=16), :]
        v_blk = kv_refs[kk][pl.ds(3 * N_KV + g, SEL_BLOCK, stride=16), :]
        ks.append(jnp.where(sub, kn, k_blk).astype(BF16))
        vs.append(jnp.where(sub, vn, v_blk).astype(BF16))
        masked = masked | ((lane_blk == jnp.where(is_new, kk, -1)) & (lane_off > 0))
    kall = jnp.concatenate(ks, axis=0)
    vall = jnp.concatenate(vs, axis=0)
    s = jnp.where(masked, NEG, _dot_nt(qg, kall))
    e = jnp.exp(s - jnp.max(s, axis=-1, keepdims=True))
    p = e / jnp.sum(e, axis=-1, keepdims=True)
    o_sel = _dot(p.astype(BF16), vall)
    gs = _sigmoid(gate_ref[0, 0])
    y_ref[0, 0] = gs[:, 0:1] * ocmp_ref[0, 0] + gs[:, 1:2] * o_sel + gs[:, 2:3] * owin_ref[0, 0]


def _dec2(pt, sidx, cache2, q4, knew4, vnew4, ocmp, owin, gate4, *, new_blk):
    nb = q4.shape[0]
    last = new_blk - 1

    def half_page(b, g, pt, si, kk):
        j = jnp.minimum(si[(b * N_KV + g) * TOPK + kk], last)
        return pt[b, lax.shift_right_logical(j, 1)] * 2 + (j & 1)

    kv_specs = [pl.BlockSpec((SEL_BLOCK * 16, 128), lambda b, g, pt, si, kk=kk: (half_page(b, g, pt, si, kk), 0))
                for kk in range(TOPK)]
    hg = pl.BlockSpec((1, 1, 8, HEAD_DIM), lambda b, g, pt, si: (b, g, 0, 0))
    return pl.pallas_call(
        functools.partial(_dec2_kernel, new_blk=new_blk),
        grid_spec=pltpu.PrefetchScalarGridSpec(
            num_scalar_prefetch=2,
            grid=(nb, N_KV),
            in_specs=kv_specs + [hg] * 6,
            out_specs=hg),
        out_shape=jax.ShapeDtypeStruct((nb, N_KV, 8, HEAD_DIM), F32),
        compiler_params=_cp(("parallel", "parallel")),
        name="dec_sel",
    )(pt, sidx, *([cache2] * TOPK), q4, knew4, vnew4, ocmp, owin, gate4)


def _pad_rows(a, rows):
    return jnp.pad(a, ((0, rows - a.shape[0]),) + ((0, 0),) * (a.ndim - 1))


def _to_hg(a2d, nb):
    g = a2d.shape[1] // (N_KV * HEAD_DIM)
    a = a2d.reshape(nb, N_KV, g, HEAD_DIM)
    return jnp.pad(a, ((0, 0), (0, 0), (0, 8 - g), (0, 0)))


def kernel(x_prompt, x_sample, cache_kv, cache_win, cache_mem, state_conv_mix, state_conv_ffn, page_table,
           mem_prompt, ln_g, ln_b, w_in_a, conv_a_w, w_in_b, w_o, w_mem_kv, w_up, ffn_conv_w, w_down,
           w_kv_shared, cmp_pos, w_cmp):
    nb, t = x_prompt.shape[:2]
    db = x_sample.shape[0]
    m_p = nb * t
    ms = 16
    tm = 512
    tpb = t // tm
    past = page_table.shape[1] * cache_kv.shape[1]

    w_in_a_b = w_in_a.astype(BF16)
    hq = N_HEADS * HEAD_DIM
    hgate = hq + 3 * N_HEADS
    wgate = w_in_b[:, :, hq:hgate].reshape(2, D_MODEL, 3, N_KV, GROUP)
    wgate = jnp.transpose(wgate, (0, 1, 3, 2, 4)).reshape(2, D_MODEL, N_KV, 3 * GROUP)
    wgate = jnp.pad(wgate, ((0, 0), (0, 0), (0, 0), (0, 128 - 3 * GROUP))).reshape(2, D_MODEL, N_KV * 128)
    w_qg_b = jnp.concatenate([w_in_b[:, :, :hq], wgate], axis=-1).astype(BF16)
    w_qm_b = jnp.concatenate([w_in_a[:, :, 3 * D_A:], w_in_b[:, :, hgate:]], axis=0).astype(BF16)
    w_o_b = w_o.astype(BF16)
    padf = D_FF_PAD - D_FF
    w_up_z = jnp.pad(w_up[:, :, :D_FF], ((0, 0), (0, 0), (0, padf))).astype(BF16)
    w_up_g = jnp.pad(w_up[:, :, D_FF:], ((0, 0), (0, 0), (0, padf))).astype(BF16)
    w_down_b = jnp.pad(w_down, ((0, 0), (0, padf), (0, 0))).astype(BF16)
    ffn_cw = jnp.pad(ffn_conv_w, ((0, 0), (0, 0), (0, padf)))
    w_kv_b = w_kv_shared.astype(BF16)
    w_mem_b = w_mem_kv.astype(BF16)
    w_cmp2 = jnp.concatenate([w_cmp[:, :CMP_STRIDE].reshape(2, 2048, 128),
                              w_cmp[:, CMP_STRIDE:].reshape(2, 2048, 128)], axis=-1).astype(BF16)
    pos8 = _posterm(jnp.broadcast_to(cmp_pos.reshape(2, 1, CMP_LEN * HEAD_DIM), (2, 8, CMP_LEN * HEAD_DIM)),
                    w_cmp.reshape(2, CMP_LEN * HEAD_DIM, HEAD_DIM))
    g2 = ln_g.reshape(2 * DEPTH, 1, D_MODEL)
    b2 = ln_b.reshape(2 * DEPTH, 1, D_MODEL)

    def overlap(n_cmp, n_lanes_j, n_rows):
        n = jnp.arange(n_rows)[:, None]
        j = jnp.arange(n_lanes_j)[None, :]
        return ((n >= 4 * j - 1) & (n <= 4 * j + 3) & (n < n_cmp)).astype(BF16)

    ovt_p = overlap(t // CMP_STRIDE - 1, 128, 128).T
    emat = (jnp.arange(t)[:, None] // SEL_BLOCK == jnp.arange(128)[None, :]).astype(BF16)
    n_half_s = past // CMP_STRIDE
    n_slc_s = past // SEL_BLOCK + 1
    ov_s = overlap(n_half_s - 1, 384, n_half_s)

    x = x_prompt.reshape(m_p, D_MODEL)
    mem_x = mem_prompt.reshape(nb * N_MEM, D_MODEL)
    mem_kv = [_mm(mem_x, w_mem_b, layer=l, tm=512, tn=2 * MEM_W, name="mm_memkv").reshape(nb, N_MEM, 2 * MEM_W)
              for l in range(DEPTH)]
    zeros_mix = jnp.zeros((nb, 8, D_A), F32)
    zeros_ffn = jnp.zeros((nb, 8, D_FF_PAD), F32)
    conv_p, ffn_p = [], []

    def tail(x, l, y, memkv_l, seq_prev8):
        x = _oproj_mem(x, y, w_qm_b, memkv_l, w_o_b, g2, b2, layer=l, tm=tm, tpb=tpb)
        x, st = _ffn_seq(x, w_up_z, w_up_g, ffn_cw, w_down_b, seq_prev8, g2, b2,
                         layer=l, tm=tm, tf=512, tpb=tpb)
        return x, st

    for l in range(N_A):
        y, st = _amix_seq(x, w_in_a_b, conv_a_w, zeros_mix, layer=l, tm=tm, tn=512, tpb=tpb)
        conv_p.append(st[tpb - 1::tpb, 6:8])
        x, fs = tail(x, l, y, mem_kv[l], zeros_ffn)
        ffn_p.append(fs[tpb - 1::tpb, 6:8, :D_FF])

    kv_tr, kvb_rows = _kvrows(x, w_kv_b, tm=tm, tn=2048)
    kv_win, kvb_win = _mm(x, w_kv_b, tm=tm, tn=1024, n=1024, col0=2, out_dtypes=(F32, BF16), name="mm_kvwin")
    pt_p = jnp.arange(m_p // PAGE, dtype=jnp.int32).reshape(nb, t // PAGE)
    ab_p = _compress(kv_tr.reshape(m_p, 16, HEAD_DIM), pt_p, w_cmp2, npg=t // PAGE)

    for j in range(DEPTH - N_A):
        l = N_A + j
        pb = _mm(x, w_qg_b, layer=j, tm=tm, tn=2048, name="mm_qgate")
        y = _nsa_prompt(pb, kvb_rows, kvb_win, ab_p, pos8, ovt_p, emat, nb=nb, t=t, tq=256, ck=512)
        x, fs = tail(x, l, y, mem_kv[l], zeros_ffn)
        ffn_p.append(fs[tpb - 1::tpb, 6:8, :D_FF])
    y_prompt = x.reshape(nb, t, D_MODEL)

    xs = _pad_rows(x_sample.reshape(db, D_MODEL), ms)
    cmem_rows = cache_mem.reshape(DEPTH * db * N_MEM * 8, HEAD_DIM)
    prev_mix = jnp.pad(state_conv_mix.transpose(0, 2, 1, 3), ((0, 0), (0, 0), (0, ms - db), (0, 0)))
    prev_ffn = jnp.pad(state_conv_ffn.transpose(0, 2, 1, 3), ((0, 0), (0, 0), (0, ms - db), (0, padf)))
    cu_s, uz_s = [], []

    def tail_s(xs, l, y):
        xrep = jnp.broadcast_to(xs[:db, None, :], (db, 8, D_MODEL))
        mo = _pad_rows(_memattn_rows(xrep, w_qm_b, cmem_rows, layer=l, tt=8, out_dtype=F32)[:, 0], ms)
        xs = _oproj(xs, y, mo, w_o_b, g2, b2, layer=l, tm=ms)
        xs, uz = _ffn_tok(xs, w_up_z, w_up_g, ffn_cw, w_down_b, prev_ffn, g2, b2, layer=l, tf=512)
        uz_s.append(uz)
        return xs

    for l in range(N_A):
        y, cu = _amix_tok(xs, w_in_a_b, conv_a_w, prev_mix, layer=l, tn=512)
        cu_s.append(cu)
        xs = tail_s(xs, l, y)

    kv_s = _mm(xs, w_kv_b, tm=ms, tn=512, name="mm_kv_tok")[:db]
    n_tok = cache_kv.shape[0] * PAGE
    ab_s = _compress(cache_kv.reshape(n_tok, 16, HEAD_DIM), page_table, w_cmp2, npg=16)
    cache2 = cache_kv.reshape(n_tok * 16, HEAD_DIM)
    cwin = cache_win.reshape(db, WINDOW, 1024)
    ksel_new = _to_hg(kv_s[:, 1024:1536], db)
    vsel_new = _to_hg(kv_s[:, 1536:2048], db)
    kwin_new = _to_hg(kv_s[:, 2048:2560], db)
    vwin_new = _to_hg(kv_s[:, 2560:3072], db)

    for j in range(DEPTH - N_A):
        l = N_A + j
        pbs = _mm(xs, w_qg_b, layer=j, tm=ms, tn=512, name="mm_qgate_tok")[:db]
        q4 = _to_hg(pbs[:, :hq], db)
        gate4 = pbs[:, hq:].reshape(db, N_KV, 128)[:, :, :3 * GROUP].reshape(db, N_KV, 3, GROUP)
        gate4 = jnp.pad(gate4.transpose(0, 1, 3, 2), ((0, 0), (0, 0), (0, 8 - GROUP), (0, 128 - 3)))
        ocmp, owin, idx = _dec1(q4, ab_s, pos8, ov_s, cwin, kwin_new, vwin_new, n_slc=n_slc_s)
        sidx = idx[:, :N_KV, :TOPK].reshape(db * N_KV * TOPK)
        y4 = _dec2(page_table, sidx, cache2, q4, ksel_new, vsel_new, ocmp, owin, gate4, new_blk=n_slc_s - 1)
        y = _pad_rows(y4[:, :, :GROUP].reshape(db, D_A), ms).astype(BF16)
        xs = tail_s(xs, l, y)
    y_sample = xs[:db].reshape(db, 1, D_MODEL)

    wb = cache_win.shape[1]
    win_new = kv_s[:, 2048:].reshape(db, 1, 2, N_KV, HEAD_DIM)
    return (y_prompt, y_sample,
            kv_tr.reshape(nb, t, 4, N_KV, HEAD_DIM),
            kv_win.reshape(nb, t, 1024)[:, max(t - WINDOW, 0):].reshape(nb, min(t, WINDOW), 2, N_KV, HEAD_DIM),
            jnp.stack(mem_kv).reshape(DEPTH, nb, N_MEM, 2, 4, HEAD_DIM),
            jnp.stack(conv_p),
            jnp.stack(ffn_p),
            kv_s[:, :2048].reshape(db, 1, 4, N_KV, HEAD_DIM),
            jnp.concatenate([cache_win, win_new], axis=1)[:, -wb:],
            jnp.stack([state_conv_mix[:, :, 1], jnp.stack(cu_s)[:, :db]], axis=2),
            jnp.stack([state_conv_ffn[:, :, 1], jnp.stack(uz_s)[:, :db, :D_FF]], axis=2))
```
